```python
import jax
import jax.numpy as jnp
from jax import lax
import numpy as np

D_MODEL = 2048
BATCH = 2
SEQ = 8192
DEPTH = 1
DEC_BATCH = 32
DEC_SEQ = 1
PAST_LEN = 16384
PAGE_SIZE = 128

D_MIX = D_MODEL
D_HGRN = D_MIX // 2
D_FOX = D_MIX - D_HGRN
HGRN_DV = 128
HGRN_HEADS = D_HGRN // HGRN_DV
HGRN_DK = 128
HGRN_K = HGRN_HEADS * HGRN_DK
FOX_HD = 128
FOX_HEADS = D_FOX // FOX_HD
HGRN_CHUNK = 64
Q_BLOCK = 128
RMS_EPS = 1e-6
ADA_SCALE = 0.2
FOX_F_BIAS = 8.0
IN_SIZES = (HGRN_K, HGRN_K, D_HGRN, D_HGRN, D_FOX, D_FOX, D_FOX, D_FOX, FOX_HEADS)
N_IN = 2 * HGRN_K + 2 * D_HGRN + 4 * D_FOX + FOX_HEADS

kernel_name = 'hymba_hgrn2_fox_step'


def rms_norm(x, w):
    xf = x.astype(jnp.float32)
    return xf * lax.rsqrt(jnp.mean(xf * xf, axis=-1, keepdims=True) + RMS_EPS) * w.astype(jnp.float32)


def split_heads(t, n):
    return t.reshape(t.shape[:-1] + (n, t.shape[-1] // n))


def mixer_in(x, c, norm_w, w_ada, b_ada, w_in, b_fox_f, lb, q_norm_w, k_norm_w):
    f32 = jnp.float32
    mod = jax.nn.silu(c.astype(f32)) @ w_ada.astype(f32) + b_ada.astype(f32)
    shift, scale, gate = jnp.split(mod, 3, axis=-1)
    h = rms_norm(x, norm_w) * (1.0 + scale[:, None, :]) + shift[:, None, :]
    z = jnp.einsum('btd,de->bte', h, w_in.astype(f32))
    idx = [int(i) for i in np.cumsum(np.array(IN_SIZES))[:-1]]
    hq, hf, hi, hg, fq, fk, fv, fg, ff = jnp.split(z, idx, axis=-1)
    a_logf = jnp.log(lb + (1.0 - lb) * jax.nn.sigmoid(hf))
    a_k = (1.0 - lb) * jax.nn.sigmoid(-hf)
    f_logf = jax.nn.log_sigmoid(ff + b_fox_f.astype(f32))
    return (gate,
            split_heads(hq, HGRN_HEADS), split_heads(a_k, HGRN_HEADS),
            split_heads(a_logf, HGRN_HEADS), split_heads(hi, HGRN_HEADS), hg,
            rms_norm(split_heads(fq, FOX_HEADS), q_norm_w),
            rms_norm(split_heads(fk, FOX_HEADS), k_norm_w),
            split_heads(fv, FOX_HEADS), f_logf, fg)


def hgrn_chunk(S0, q, k, v, g):
    L = q.shape[2]
    b = jnp.cumsum(g, axis=2)
    causal = jnp.tril(jnp.ones((L, L), dtype=bool))
    diff = b[:, :, :, None, :] - b[:, :, None, :, :]
    decay = jnp.exp(jnp.where(causal[None, None, :, :, None], diff, -jnp.inf))
    A = jnp.einsum('bhtd,bhsd,bhtsd->bhts', q, k, decay)
    o = (jnp.einsum('bhtd,bhdv->bhtv', q * jnp.exp(b), S0)
         + jnp.einsum('bhts,bhsv->bhtv', A, v))
    b_last = b[:, :, -1:, :]
    S = (jnp.exp(b_last[:, :, 0, :, None]) * S0
         + jnp.einsum('bhsd,bhsv->bhdv', k * jnp.exp(b_last - b), v))
    return S, o


def hgrn_prompt(q, k, logf, v):
    B, T, H, _ = q.shape
    nc = T // HGRN_CHUNK

    def to_chunks(t):
        return t.reshape(B, nc, HGRN_CHUNK, H, t.shape[-1]).transpose(1, 0, 3, 2, 4)

    S0 = jnp.zeros((B, H, HGRN_DK, HGRN_DV), jnp.float32)

    def step(S, inp):
        cq, ck, cv, cg = inp
        return hgrn_chunk(S, cq, ck, cv, cg)

    S, o = lax.scan(step, S0, (to_chunks(q), to_chunks(k), to_chunks(v), to_chunks(logf)))
    o = o.transpose(1, 0, 3, 2, 4).reshape(B, T, H, HGRN_DV)
    return S, o


def hgrn_sample(S0, q, k, logf, v):
    tr = lambda t: t.transpose(0, 2, 1, 3)
    S, o = hgrn_chunk(S0.astype(jnp.float32), tr(q), tr(k), tr(v), tr(logf))
    return S, tr(o)


def fox_prompt(q, k, v, logf):
    B, T, H, HD = q.shape
    scale = HD ** -0.5
    cT = jnp.cumsum(logf, axis=1).transpose(0, 2, 1)
    kpos = jnp.arange(T)

    def block(i):
        s0 = i * Q_BLOCK
        qb = lax.dynamic_slice_in_dim(q, s0, Q_BLOCK, axis=1)
        cb = lax.dynamic_slice_in_dim(cT, s0, Q_BLOCK, axis=2)
        qpos = s0 + jnp.arange(Q_BLOCK)
        logits = (jnp.einsum('bqhd,bkhd->bhqk', qb, k) * scale
                  + cb[:, :, :, None] - cT[:, :, None, :])
        logits = jnp.where(kpos[None, :] <= qpos[:, None], logits, -jnp.inf)
        p = jax.nn.softmax(logits, axis=-1)
        return jnp.einsum('bhqk,bkhd->bqhd', p, v)

    out = lax.map(block, jnp.arange(T // Q_BLOCK))
    return out.transpose(1, 0, 2, 3, 4).reshape(B, T, H, HD)


def fox_sample(q, k, v, logf, k_past, v_past, logf_past):
    T = q.shape[1]
    P = k_past.shape[1]
    scale = q.shape[-1] ** -0.5
    lp = logf_past.astype(jnp.float32)
    r = (lax.cumsum(lp, axis=1, reverse=True) - lp).transpose(0, 2, 1)
    cn = jnp.cumsum(logf, axis=1).transpose(0, 2, 1)
    s_past = (jnp.einsum('bqhd,bkhd->bhqk', q, k_past.astype(jnp.float32)) * scale
              + cn[:, :, :, None] + r[:, :, None, :])
    s_new = (jnp.einsum('bqhd,bkhd->bhqk', q, k) * scale
             + cn[:, :, :, None] - cn[:, :, None, :])
    causal = jnp.tril(jnp.ones((T, T), dtype=bool))
    s_new = jnp.where(causal, s_new, -jnp.inf)
    p = jax.nn.softmax(jnp.concatenate([s_past, s_new], axis=-1), axis=-1)
    return (jnp.einsum('bhqk,bkhd->bqhd', p[..., :P], v_past.astype(jnp.float32))
            + jnp.einsum('bhqk,bkhd->bqhd', p[..., P:], v))


def mixer_out(x, gate, o_hgrn, hg, o_fox, fg, hgrn_norm_w, w_out):
    B, T = x.shape[0], x.shape[1]
    a = rms_norm(o_hgrn, hgrn_norm_w).reshape(B, T, D_HGRN) * jax.nn.silu(hg)
    b = o_fox.reshape(B, T, D_FOX) * jax.nn.silu(fg)
    mix = jnp.concatenate([a, b], axis=-1)
    out = jnp.einsum('bte,ed->btd', mix, w_out.astype(jnp.float32))
    return (x.astype(jnp.float32) + gate[:, None, :] * out).astype(x.dtype)


def setup_inputs(seed: int = 0) -> dict:
    key = jax.random.key(seed)
    ks = jax.random.split(key, 24)
    f32 = jnp.float32
    n_pages = PAST_LEN // PAGE_SIZE
    n_used = DEC_BATCH * n_pages
    n_phys = (5 * n_used + 3) // 4
    perm = jax.random.permutation(ks[0], n_phys)
    page_table = perm[:n_used].reshape(DEC_BATCH, n_pages).astype(jnp.int32)
    nrm = lambda k, s: jax.random.normal(k, s, f32)
    return {
        'x_prompt': nrm(ks[1], (BATCH, SEQ, D_MODEL)),
        'x_sample': nrm(ks[2], (DEC_BATCH, DEC_SEQ, D_MODEL)),
        'c_prompt': nrm(ks[3], (BATCH, D_MODEL)),
        'c_sample': nrm(ks[4], (DEC_BATCH, D_MODEL)),
        'cache_k': nrm(ks[5], (DEPTH, n_phys, PAGE_SIZE, FOX_HEADS, FOX_HD)),
        'cache_v': nrm(ks[6], (DEPTH, n_phys, PAGE_SIZE, FOX_HEADS, FOX_HD)),
        'cache_logf': jax.nn.log_sigmoid(FOX_F_BIAS + nrm(ks[7], (DEPTH, n_phys, PAGE_SIZE, FOX_HEADS))),
        'state_hgrn': 0.5 * nrm(ks[8], (DEPTH, DEC_BATCH, HGRN_HEADS, HGRN_DK, HGRN_DV)),
        'page_table': page_table,
        'norm_w': 1.0 + 0.02 * nrm(ks[9], (DEPTH, D_MODEL)),
        'w_ada': nrm(ks[10], (DEPTH, D_MODEL, 3 * D_MODEL)) * (D_MODEL ** -0.5) * ADA_SCALE,
        'b_ada': 0.01 * nrm(ks[11], (DEPTH, 3 * D_MODEL)),
        'w_in': nrm(ks[12], (DEPTH, D_MODEL, N_IN)) * (D_MODEL ** -0.5),
        'b_fox_f': FOX_F_BIAS + 0.5 * nrm(ks[13], (DEPTH, FOX_HEADS)),
        'lb_logits': 0.1 * nrm(ks[14], (DEPTH + 1, HGRN_K)),
        'q_norm_w': 1.0 + 0.02 * nrm(ks[15], (DEPTH, FOX_HD)),
        'k_norm_w': 1.0 + 0.02 * nrm(ks[16], (DEPTH, FOX_HD)),
        'hgrn_norm_w': 1.0 + 0.02 * nrm(ks[17], (DEPTH, HGRN_DV)),
        'w_out': nrm(ks[18], (DEPTH, D_MIX, D_MODEL)) * (D_MIX ** -0.5),
    }


def reference(x_prompt, x_sample, c_prompt, c_sample, cache_k, cache_v, cache_logf, state_hgrn,
              page_table, norm_w, w_ada, b_ada, w_in, b_fox_f, lb_logits, q_norm_w, k_norm_w,
              hgrn_norm_w, w_out):
    lb_all = jnp.cumsum(jax.nn.softmax(lb_logits.astype(jnp.float32), axis=0), axis=0)
    db, n_pg = page_table.shape
    past = n_pg * cache_k.shape[2]
    nkp, nvp, nlp, nsp = [], [], [], []
    nks, nvs, nls, nss = [], [], [], []
    for l in range(DEPTH):
        lb = lb_all[l]
        (gp, hq, hk, hlf, hi, hg, fq, fk, fv, flf, fg) = mixer_in(
            x_prompt, c_prompt, norm_w[l], w_ada[l], b_ada[l], w_in[l], b_fox_f[l], lb,
            q_norm_w[l], k_norm_w[l])
        s_p, o_h = hgrn_prompt(hq, hk, hlf, hi)
        o_f = fox_prompt(fq, fk, fv, flf)
        x_prompt = mixer_out(x_prompt, gp, o_h, hg, o_f, fg, hgrn_norm_w[l], w_out[l])
        nkp.append(fk.astype(cache_k.dtype))
        nvp.append(fv.astype(cache_v.dtype))
        nlp.append(flf.astype(cache_logf.dtype))
        nsp.append(s_p.astype(state_hgrn.dtype))
        (gs, sq, sk, slf, si, sg, tq, tk, tv, tlf, tg) = mixer_in(
            x_sample, c_sample, norm_w[l], w_ada[l], b_ada[l], w_in[l], b_fox_f[l], lb,
            q_norm_w[l], k_norm_w[l])
        s_s, o_hs = hgrn_sample(state_hgrn[l], sq, sk, slf, si)
        k_past = cache_k[l][page_table].reshape(db, past, FOX_HEADS, FOX_HD)
        v_past = cache_v[l][page_table].reshape(db, past, FOX_HEADS, FOX_HD)
        lf_past = cache_logf[l][page_table].reshape(db, past, FOX_HEADS)
        o_fs = fox_sample(tq, tk, tv, tlf, k_past, v_past, lf_past)
        x_sample = mixer_out(x_sample, gs, o_hs, sg, o_fs, tg, hgrn_norm_w[l], w_out[l])
        nks.append(tk.astype(cache_k.dtype))
        nvs.append(tv.astype(cache_v.dtype))
        nls.append(tlf.astype(cache_logf.dtype))
        nss.append(s_s.astype(state_hgrn.dtype))
    y_prompt = x_prompt
    y_sample = x_sample
    return (y_prompt, y_sample, jnp.stack(nkp), jnp.stack(nvp), jnp.stack(nlp), jnp.stack(nsp),
            jnp.stack(nks), jnp.stack(nvs), jnp.stack(nls), jnp.stack(nss))
```

```python
import functools

import jax
import jax.numpy as jnp
import numpy as np
from jax import lax
from jax.experimental import pallas as pl
from jax.experimental.pallas import tpu as pltpu

F32 = jnp.float32
BF16 = jnp.bfloat16

LANES = 128
VMEM_LIMIT = 56 * 1024 * 1024
RMS_EPS = 1e-6
LOG2E = 1.4426950408889634
HGRN_CHUNK = 64
ROW_TILE = 512
ATTN_TILE = 512
PAGES_PER_STEP = 8


def _cparams(*sem):
    return pltpu.CompilerParams(dimension_semantics=sem, vmem_limit_bytes=VMEM_LIMIT)


def _dot(a, b):
    return jnp.dot(a, b, preferred_element_type=F32)


def _dot_nt(a, b):
    return lax.dot_general(a, b, (((1,), (1,)), ((), ())), preferred_element_type=F32)


def _dot_tn(a, b):
    return lax.dot_general(a, b, (((0,), (0,)), ((), ())), preferred_element_type=F32)


def _split3(x):
    hi = x.astype(BF16)
    r = x - hi.astype(F32)
    mid = r.astype(BF16)
    lo = (r - mid.astype(F32)).astype(BF16)
    return hi, mid, lo


def _dot01(m01, x):
    hi, mid, lo = _split3(x)
    return _dot(m01, hi) + _dot(m01, mid) + _dot(m01, lo)


def _dot01_nt(x, m01):
    hi, mid, lo = _split3(x)
    return _dot(hi, m01) + _dot(mid, m01) + _dot(lo, m01)


def _sigmoid(x):
    return 1.0 / (1.0 + jnp.exp(-x))


def _silu(x):
    return x * _sigmoid(x)


def _rms(x, w):
    ms = jnp.mean(x * x, axis=-1, keepdims=True)
    return x * lax.rsqrt(ms + RMS_EPS) * w


def _ada_kernel(c_ref, w_ref, b_ref, o_ref):
    s = _silu(c_ref[...])
    s_hi = s.astype(BF16)
    s_lo = (s - s_hi.astype(F32)).astype(BF16)
    w = w_ref[...]
    w_hi = w.astype(BF16)
    w_lo = (w - w_hi.astype(F32)).astype(BF16)
    o_ref[...] = _dot(s_hi, w_hi) + _dot(s_hi, w_lo) + _dot(s_lo, w_hi) + b_ref[...]


def _ada_call(c_all, w_ada, b_ada):
    r, d = c_all.shape
    n = w_ada.shape[1]
    tn = 512
    return pl.pallas_call(
        _ada_kernel,
        grid=(n // tn,),
        in_specs=[pl.BlockSpec((r, d), lambda j: (0, 0)),
                  pl.BlockSpec((d, tn), lambda j: (0, j)),
                  pl.BlockSpec((1, tn), lambda j: (0, j))],
        out_specs=pl.BlockSpec((r, tn), lambda j: (0, j)),
        out_shape=jax.ShapeDtypeStruct((r, n), F32),
        compiler_params=_cparams("parallel"),
        name="ada_mod",
    )(c_all, w_ada, b_ada.reshape(1, n))


def _norm_kernel(x_ref, nw_ref, shift_ref, scale_ref, h_ref):
    xn = _rms(x_ref[...], nw_ref[...])
    h_ref[...] = (xn * (1.0 + scale_ref[...]) + shift_ref[...]).astype(BF16)


def _mod_spec(mod3, d, col, tiles_per_group):
    return pl.BlockSpec((None, mod3.shape[1], d), lambda i: (i // tiles_per_group, 0, col))


def _norm_call(x2, norm_w, mod3, tm, tiles_per_group):
    m, d = x2.shape
    return pl.pallas_call(
        _norm_kernel,
        grid=(m // tm,),
        in_specs=[pl.BlockSpec((tm, d), lambda i: (i, 0)),
                  pl.BlockSpec((1, d), lambda i: (0, 0)),
                  _mod_spec(mod3, d, 0, tiles_per_group),
                  _mod_spec(mod3, d, 1, tiles_per_group)],
        out_specs=pl.BlockSpec((tm, d), lambda i: (i, 0)),
        out_shape=jax.ShapeDtypeStruct((m, d), BF16),
        compiler_params=_cparams("parallel"),
        name="norm_mod",
    )(x2, norm_w.reshape(1, d), mod3, mod3)


def _proj_plain_kernel(h_ref, w_ref, o_ref, *, act):
    z = _dot(h_ref[...], w_ref[...])
    if act:
        z = _silu(z)
    o_ref[...] = z.astype(o_ref.dtype)


def _proj_v_kernel(h_ref, w_ref, o_ref, ob_ref):
    z = _dot(h_ref[...], w_ref[...])
    o_ref[...] = z
    ob_ref[...] = z.astype(BF16)


def _proj_forget_kernel(h_ref, w_ref, lbl_ref, logf_ref, k_ref, *, layer):
    z = _dot(h_ref[...], w_ref[...])
    lbl = lbl_ref[...]
    e = jnp.exp(lbl - jnp.max(lbl, axis=0, keepdims=True))
    lb = jnp.sum(e[:layer + 1], axis=0, keepdims=True) / jnp.sum(e, axis=0, keepdims=True)
    logf_ref[...] = jnp.log(lb + (1.0 - lb) * _sigmoid(z))
    k_ref[...] = ((1.0 - lb) * _sigmoid(-z)).astype(BF16)


def _aug_lanes(c2, is_query, tm):
    hi, mid, lo = _split3(c2)
    hi, mid, lo = hi.astype(F32), mid.astype(F32), lo.astype(F32)
    lane = lax.broadcasted_iota(jnp.int32, (tm, LANES), 1)
    one = jnp.ones((tm, LANES), F32)
    zero = jnp.zeros((tm, LANES), F32)
    if is_query:
        parts = jnp.where(lane == 0, hi, jnp.where(lane == 1, mid, jnp.where(lane == 2, lo, zero)))
        return jnp.where((lane >= 3) & (lane < 6), one, parts).astype(BF16)
    parts = jnp.where(lane == 3, -hi, jnp.where(lane == 4, -mid, jnp.where(lane == 5, -lo, zero)))
    return jnp.where(lane < 3, one, parts).astype(BF16)


def _proj_qk_kernel(h_ref, w_ref, nw_ref, c_ref, *out_refs, heads, hd, is_query, qscale):
    z = _dot(h_ref[...], w_ref[...])
    tm = z.shape[0]
    nw = nw_ref[...]
    c = c_ref[...]
    for h in range(heads):
        n = _rms(z[:, h * hd:(h + 1) * hd], nw)
        aug = _aug_lanes(c[:, h:h + 1] * LOG2E, is_query, tm)
        if is_query:
            (aug_ref,) = out_refs
            n = n * qscale
        else:
            full_ref, aug_ref = out_refs
            full_ref[:, h * hd:(h + 1) * hd] = n
        aug_ref[:, 2 * h * hd:(2 * h + 1) * hd] = n.astype(BF16)
        aug_ref[:, (2 * h + 1) * hd:(2 * h + 2) * hd] = aug


def _proj_fgate_kernel(h_ref, w_ref, b_ref, logf_ref, c_ref, carry_ref, *, heads, tiles_per_seq):
    x = _dot(h_ref[...], w_ref[...]) + b_ref[...]
    lf = jnp.minimum(x, 0.0) - jnp.log1p(jnp.exp(-jnp.abs(x)))
    logf_ref[...] = lf[:, :heads]
    if tiles_per_seq is None:
        c_ref[...] = lf[:, :heads]
        return
    tm = lf.shape[0]

    @pl.when(pl.program_id(0) % tiles_per_seq == 0)
    def _():
        carry_ref[...] = jnp.zeros_like(carry_ref)

    row = lax.broadcasted_iota(jnp.int32, (tm, tm), 0)
    col = lax.broadcasted_iota(jnp.int32, (tm, tm), 1)
    tril = jnp.where(col <= row, 1.0, 0.0).astype(BF16)
    c = _dot01(tril, lf) + carry_ref[...]
    c_ref[...] = c[:, :heads]
    carry_ref[...] = c[tm - 1:tm, :]


def _proj_call(kernel, h, w, seg_w, seg, extra_inputs, extra_specs, out_shapes, out_specs, tm,
               scratch=(), sem="parallel", name="in_proj"):
    m, d = h.shape
    return pl.pallas_call(
        kernel,
        grid=(m // tm,),
        in_specs=[pl.BlockSpec((tm, d), lambda i: (i, 0)),
                  pl.BlockSpec((d, seg_w), lambda i: (0, seg))] + list(extra_specs),
        out_specs=out_specs,
        out_shape=out_shapes,
        scratch_shapes=list(scratch),
        compiler_params=_cparams(sem),
        name=name,
    )(h, w, *extra_inputs)


def _full_spec(shape):
    nd = len(shape)
    return pl.BlockSpec(shape, lambda i: (0,) * nd)


def _in_proj(h, w_bf, w_ff, b_ff, lb_logits, q_norm_w, k_norm_w, layer, tm, seq_len,
             hgrn_heads, dk, fox_heads, hd):
    m, d = h.shape
    hk = hgrn_heads * dk
    dfox = fox_heads * hd
    row = lambda width: pl.BlockSpec((tm, width), lambda i: (i, 0))
    sds = lambda width, dt: jax.ShapeDtypeStruct((m, width), dt)
    assert hk == dfox, "segments are indexed as equal-width column blocks"
    plain = lambda seg, act, nm: _proj_call(
        functools.partial(_proj_plain_kernel, act=act), h, w_bf, hk, seg, (), (),
        sds(hk, BF16), row(hk), tm, name=nm)
    hq = plain(0, False, "proj_hq")
    a_logf, a_k = _proj_call(
        functools.partial(_proj_forget_kernel, layer=layer), h, w_bf, hk, 1,
        (lb_logits,), (_full_spec(lb_logits.shape),),
        (sds(hk, F32), sds(hk, BF16)), (row(hk), row(hk)), tm, name="proj_hf")
    hi = plain(2, False, "proj_hi")
    sg_h = plain(3, True, "proj_hg")
    tiles_per_seq = None if seq_len == 1 else seq_len // tm
    f_logf, f_c = _proj_call(
        functools.partial(_proj_fgate_kernel, heads=fox_heads, tiles_per_seq=tiles_per_seq),
        h, w_ff, LANES, 0, (b_ff,), (_full_spec(b_ff.shape),),
        (sds(fox_heads, F32), sds(fox_heads, F32)), (row(fox_heads), row(fox_heads)), tm,
        scratch=(pltpu.VMEM((1, LANES), F32),), sem="arbitrary", name="proj_ff")
    qk = lambda seg, nw, is_query, outs, specs, nm: _proj_call(
        functools.partial(_proj_qk_kernel, heads=fox_heads, hd=hd, is_query=is_query,
                          qscale=hd ** -0.5 * LOG2E),
        h, w_bf, dfox, seg, (nw.reshape(1, hd), f_c), (_full_spec((1, hd)), row(fox_heads)),
        outs, specs, tm, name=nm)
    q_aug = qk(4, q_norm_w, True, sds(2 * dfox, BF16), row(2 * dfox), "proj_fq")
    f_k, k_aug = qk(5, k_norm_w, False, (sds(dfox, F32), sds(2 * dfox, BF16)),
                    (row(dfox), row(2 * dfox)), "proj_fk")
    f_v, v_bf = _proj_call(_proj_v_kernel, h, w_bf, dfox, 6, (), (),
                           (sds(dfox, F32), sds(dfox, BF16)), (row(dfox), row(dfox)), tm,
                           name="proj_fv")
    sg_f = plain(7, True, "proj_fg")
    return dict(hq=hq, a_logf=a_logf, a_k=a_k, hi=hi, sg_h=sg_h, f_logf=f_logf, f_c=f_c,
                q_aug=q_aug, f_k=f_k, k_aug=k_aug, f_v=f_v, v_bf=v_bf, sg_f=sg_f)


def _hgrn_levels(c):
    levels = []
    m = c // 2
    while m >= 1:
        levels.append(m)
        m //= 2
    return levels


def _hgrn_constants(c):
    levels = _hgrn_levels(c)
    nl = len(levels)
    mall = np.zeros(((nl + 1) * c, c), np.float32)
    sel = np.zeros((nl * c, LANES), np.float32)
    mask = np.zeros((nl * c, c), np.float32)
    for t in range(c):
        mall[t, :t + 1] = 1.0
    for li, m in enumerate(levels):
        for t in range(c):
            blk, pos = divmod(t, 2 * m)
            mid = blk * 2 * m + m
            r = (li + 1) * c + t
            if pos >= m:
                mall[r, mid:t + 1] = 1.0
                sel[li * c + t, :] = 1.0
                mask[li * c + t, blk * 2 * m:mid] = 1.0
            else:
                mall[r, t + 1:mid] = 1.0
    return (jnp.asarray(mall, BF16), jnp.asarray(sel, F32), jnp.asarray(mask, F32)), nl


def _hgrn_kernel(q_ref, k_ref, g_ref, v_ref, sg_ref, nw_ref, mall_ref, sel_ref, mask_ref,
                 o_ref, s_ref, st_ref, *, c, nl):
    ci = pl.program_id(2)

    @pl.when(ci == 0)
    def _():
        st_ref[...] = jnp.zeros_like(st_ref)

    q = q_ref[...].astype(F32)
    k = k_ref[...].astype(F32)
    v = v_ref[...]
    e_all = _dot01(mall_ref[...], g_ref[...])
    b = e_all[:c]
    b_last = b[c - 1:c]
    row = lax.broadcasted_iota(jnp.int32, (c, c), 0)
    col = lax.broadcasted_iota(jnp.int32, (c, c), 1)
    a = jnp.where(row == col, _dot_nt(q.astype(BF16), k.astype(BF16)), 0.0)
    for li in range(nl):
        ex = jnp.exp(e_all[(li + 1) * c:(li + 2) * c])
        x = (jnp.where(sel_ref[li * c:(li + 1) * c, :] > 0.5, q, k) * ex).astype(BF16)
        a = a + _dot_nt(x, x) * mask_ref[li * c:(li + 1) * c, :]
    st = st_ref[...]
    o = _dot_nt((q * jnp.exp(b)).astype(BF16), st.astype(BF16)) + _dot(a.astype(BF16), v)
    kd = (k * jnp.exp(b_last - b)).astype(BF16)
    st_new = st * jnp.exp(b_last) + _dot_tn(v, kd)
    st_ref[...] = st_new
    o_ref[...] = (_rms(o, nw_ref[...]) * sg_ref[...].astype(F32)).astype(BF16)

    @pl.when(ci == pl.num_programs(2) - 1)
    def _():
        s_ref[...] = st_new.T


def _hgrn_prompt_call(p, hgrn_norm_w, batch, seq, heads, dk, dv):
    c = HGRN_CHUNK
    consts, nl = _hgrn_constants(c)
    nc = seq // c
    blk = lambda width: pl.BlockSpec((c, width), lambda b, h, i: (b * nc + i, h))
    return pl.pallas_call(
        functools.partial(_hgrn_kernel, c=c, nl=nl),
        grid=(batch, heads, nc),
        in_specs=[blk(dk), blk(dk), blk(dk), blk(dv), blk(dv),
                  pl.BlockSpec((1, dv), lambda b, h, i: (0, 0))]
                 + [pl.BlockSpec(x.shape, lambda b, h, i: (0, 0)) for x in consts],
        out_specs=(blk(dv), pl.BlockSpec((None, None, dk, dv), lambda b, h, i: (b, h, 0, 0))),
        out_shape=(jax.ShapeDtypeStruct((batch * seq, heads * dv), BF16),
                   jax.ShapeDtypeStruct((batch, heads, dk, dv), F32)),
        scratch_shapes=[pltpu.VMEM((dv, dk), F32)],
        compiler_params=_cparams("parallel", "parallel", "arbitrary"),
        name="hgrn_prompt",
    )(p["hq"], p["a_k"], p["a_logf"], p["hi"], p["sg_h"], hgrn_norm_w.reshape(1, dv), *consts)


def _hgrn_step_kernel(q_ref, k_ref, g_ref, v_ref, sg_ref, nw_ref, s0_ref, o_ref, s_ref, *, heads):
    pad = lambda x: jnp.concatenate([x, jnp.zeros((LANES - heads, x.shape[1]), F32)], axis=0)
    qt = pad(q_ref[...].astype(F32)).T
    kt = pad(k_ref[...].astype(F32)).T
    ft = pad(jnp.exp(g_ref[...])).T
    v = v_ref[...].astype(F32)
    sg = sg_ref[...].astype(F32)
    rows = []
    for h in range(heads):
        s = ft[:, h:h + 1] * s0_ref[h] + kt[:, h:h + 1] * v[h:h + 1, :]
        s_ref[h] = s
        o = jnp.sum(qt[:, h:h + 1] * s, axis=0, keepdims=True)
        rows.append(_rms(o, nw_ref[...]) * sg[h:h + 1, :])
    o_ref[...] = jnp.concatenate(rows, axis=0).astype(BF16)


def _hgrn_step_call(p, hgrn_norm_w, state, heads, dk, dv):
    db = state.shape[0]
    r3 = lambda x, w: x.reshape(db, heads, w)
    blk = lambda w: pl.BlockSpec((None, heads, w), lambda b: (b, 0, 0))
    st_blk = pl.BlockSpec((None, heads, dk, dv), lambda b: (b, 0, 0, 0))
    o, s = pl.pallas_call(
        functools.partial(_hgrn_step_kernel, heads=heads),
        grid=(db,),
        in_specs=[blk(dk), blk(dk), blk(dk), blk(dv), blk(dv),
                  pl.BlockSpec((1, dv), lambda b: (0, 0)), st_blk],
        out_specs=(blk(dv), st_blk),
        out_shape=(jax.ShapeDtypeStruct((db, heads, dv), BF16),
                   jax.ShapeDtypeStruct(state.shape, F32)),
        compiler_params=_cparams("parallel"),
        name="hgrn_step",
    )(r3(p["hq"], dk), r3(p["a_k"], dk), r3(p["a_logf"], dk), r3(p["hi"], dv), r3(p["sg_h"], dv),
      hgrn_norm_w.reshape(1, dv), state)
    return o.reshape(db, heads * dv), s


def _fox_kernel(q_ref, k_ref, v_ref, sg_ref, o_ref, m_ref, l_ref, acc_ref, *, t):
    qi = pl.program_id(2)
    q = q_ref[...]
    m_ref[...] = jnp.full_like(m_ref, -jnp.inf)
    l_ref[...] = jnp.zeros_like(l_ref)
    acc_ref[...] = jnp.zeros_like(acc_ref)

    def step(j, masked):
        start = pl.multiple_of(j * t, t)
        s = _dot_nt(q, k_ref[pl.ds(start, t), :])
        if masked:
            row = lax.broadcasted_iota(jnp.int32, (t, t), 0)
            col = lax.broadcasted_iota(jnp.int32, (t, t), 1)
            s = jnp.where(col <= row, s, -jnp.inf)
        m_prev = m_ref[...]
        m_new = jnp.maximum(m_prev, jnp.max(s, axis=-1, keepdims=True))
        alpha = jnp.exp2(m_prev - m_new)
        p = jnp.exp2(s - m_new)
        l_ref[...] = alpha * l_ref[...] + jnp.sum(p, axis=-1, keepdims=True)
        acc_ref[...] = alpha * acc_ref[...] + _dot(p.astype(BF16), v_ref[pl.ds(start, t), :])
        m_ref[...] = m_new

    def body(j, carry):
        step(j, False)
        return carry

    lax.fori_loop(0, qi, body, 0)
    step(qi, True)
    o_ref[...] = (acc_ref[...] / l_ref[...] * sg_ref[...].astype(F32)).astype(BF16)


def _fox_prompt_call(p, batch, seq, heads, hd):
    t = ATTN_TILE
    nq = seq // t
    return pl.pallas_call(
        functools.partial(_fox_kernel, t=t),
        grid=(batch, heads, nq),
        in_specs=[pl.BlockSpec((t, 2 * hd), lambda b, h, i: (b * nq + i, h)),
                  pl.BlockSpec((seq, 2 * hd), lambda b, h, i: (b, h)),
                  pl.BlockSpec((seq, hd), lambda b, h, i: (b, h)),
                  pl.BlockSpec((t, hd), lambda b, h, i: (b * nq + i, h))],
        out_specs=pl.BlockSpec((t, hd), lambda b, h, i: (b * nq + i, h)),
        out_shape=jax.ShapeDtypeStruct((batch * seq, heads * hd), BF16),
        scratch_shapes=[pltpu.VMEM((t, 1), F32), pltpu.VMEM((t, 1), F32), pltpu.VMEM((t, hd), F32)],
        compiler_params=_cparams("parallel", "parallel", "arbitrary"),
        name="fox_prompt",
    )(p["q_aug"], p["k_aug"], p["v_bf"], p["sg_f"])


HEAD_PAD = 16


def _decode_kernel(pt_ref, qbd_ref, q3_ref, k3_ref, cn_ref, vnew_ref, sg_ref, *refs,
                   pps, heads, hd):
    k_refs = refs[:pps]
    v_refs = refs[pps:2 * pps]
    lf_refs = refs[2 * pps:3 * pps]
    o_ref, m_ref, l_ref, r_ref, acc_ref = refs[3 * pps:]
    j = pl.program_id(1)
    rows = k_refs[0].shape[0]

    @pl.when(j == 0)
    def _():
        m_ref[...] = jnp.full_like(m_ref, -jnp.inf)
        l_ref[...] = jnp.zeros_like(l_ref)
        r_ref[...] = jnp.zeros_like(r_ref)
        acc_ref[...] = jnp.zeros_like(acc_ref)

    qbd = qbd_ref[...]
    cn = cn_ref[...]
    jr = lax.broadcasted_iota(jnp.int32, (rows, rows), 0)
    rr = lax.broadcasted_iota(jnp.int32, (rows, rows), 1)
    later = jnp.where(jr > rr, 1.0, 0.0).astype(BF16)
    zpad = jnp.zeros((HEAD_PAD - heads, rows), F32)

    run = r_ref[...]
    scores = []
    for p in range(pps):
        lf = jnp.concatenate([lf_refs[p][...], zpad], axis=0)
        bias = (_dot01_nt(lf, later) + run + cn) * LOG2E
        run = run + jnp.sum(lf, axis=-1, keepdims=True)
        scores.append(_dot_nt(qbd, k_refs[p][...].astype(BF16)) + bias)
    r_ref[...] = run

    m_prev = m_ref[...]
    m_new = m_prev
    for s in scores:
        m_new = jnp.maximum(m_new, jnp.max(s, axis=-1, keepdims=True))
    alpha = jnp.exp2(m_prev - m_new)
    l_new = alpha * l_ref[...]
    acc = alpha[:, :1] * acc_ref[...]
    for p in range(pps):
        pr = jnp.exp2(scores[p] - m_new)
        l_new = l_new + jnp.sum(pr, axis=-1, keepdims=True)
        acc = acc + _dot(pr.astype(BF16), v_refs[p][...].astype(BF16))
    m_ref[...] = m_new
    l_ref[...] = l_new
    acc_ref[...] = acc

    @pl.when(j == pl.num_programs(1) - 1)
    def _():
        s_new = jnp.sum(q3_ref[...].astype(F32) * k3_ref[...].astype(F32), axis=-1, keepdims=True)
        m_fin = jnp.maximum(m_new, s_new)
        a_fin = jnp.exp2(m_new - m_fin)
        p_new = jnp.exp2(s_new - m_fin)
        l_fin = a_fin * l_new + p_new
        acc_fin = a_fin[:, :1] * acc + p_new[:, :1] * vnew_ref[...].astype(F32)
        out = [acc_fin[h:h + 1, h * hd:(h + 1) * hd] / l_fin[h:h + 1, :1] for h in range(heads)]
        o_ref[...] = (jnp.concatenate(out, axis=1) * sg_ref[...].astype(F32)).astype(BF16)


def _decode_call(p, cache_k, cache_v, cache_lf_t, page_table, layer, heads, hd):
    db, n_pg = page_table.shape
    rows = cache_k.shape[2]
    pps = PAGES_PER_STEP
    d = heads * hd
    qn = p["q_aug"].reshape(db, heads, 2 * hd)[:, :, :hd]
    eye = jnp.eye(HEAD_PAD, heads, dtype=BF16)
    qbd = (eye[None, :, :, None] * qn[:, None, :, :]).reshape(db, HEAD_PAD, d)
    padh = lambda x: jnp.pad(x, ((0, 0), (0, HEAD_PAD - heads), (0, 0)))
    q3 = padh(qn)
    k3 = padh(p["k_aug"].reshape(db, heads, 2 * hd)[:, :, :hd])
    cn = jnp.broadcast_to(padh(p["f_logf"][:, :, None]), (db, HEAD_PAD, LANES))
    vnew = p["v_bf"].reshape(db, 1, d)
    sg = p["sg_f"].reshape(db, 1, d)

    seq_blk = lambda shape: pl.BlockSpec((None,) + shape, lambda b, j, pt: (b, 0, 0))

    def page_blk(shape, slot):
        return pl.BlockSpec(
            (None, None) + shape,
            lambda b, j, pt: (layer, pt[b, n_pg - 1 - (j * pps + slot)], 0, 0))

    in_specs = ([seq_blk((HEAD_PAD, d)), seq_blk((HEAD_PAD, hd)), seq_blk((HEAD_PAD, hd)),
                 seq_blk((HEAD_PAD, LANES)), seq_blk((1, d)), seq_blk((1, d))]
                + [page_blk((rows, d), s) for s in range(pps)]
                + [page_blk((rows, d), s) for s in range(pps)]
                + [page_blk((heads, rows), s) for s in range(pps)])
    grid_spec = pltpu.PrefetchScalarGridSpec(
        num_scalar_prefetch=1,
        grid=(db, n_pg // pps),
        in_specs=in_specs,
        out_specs=pl.BlockSpec((None, 1, d), lambda b, j, pt: (b, 0, 0)),
        scratch_shapes=[pltpu.VMEM((HEAD_PAD, LANES), F32), pltpu.VMEM((HEAD_PAD, LANES), F32),
                        pltpu.VMEM((HEAD_PAD, LANES), F32), pltpu.VMEM((HEAD_PAD, d), F32)])
    out = pl.pallas_call(
        functools.partial(_decode_kernel, pps=pps, heads=heads, hd=hd),
        grid_spec=grid_spec,
        out_shape=jax.ShapeDtypeStruct((db, 1, d), BF16),
        compiler_params=_cparams("parallel", "arbitrary"),
        name="fox_decode",
    )(page_table, qbd, q3, k3, cn, vnew, sg,
      *([cache_k] * pps), *([cache_v] * pps), *([cache_lf_t] * pps))
    return out.reshape(db, d)


def _out_kernel(a_ref, b_ref, wa_ref, wb_ref, x_ref, gate_ref, y_ref):
    out = _dot(a_ref[...], wa_ref[...]) + _dot(b_ref[...], wb_ref[...])
    y_ref[...] = x_ref[...] + gate_ref[...] * out


def _out_call(a, b, w_out_bf, x2, mod3, tm, tiles_per_group):
    m, d = x2.shape
    da, db_ = a.shape[1], b.shape[1]
    assert da == db_, "w_out is read as two equal row blocks"
    return pl.pallas_call(
        _out_kernel,
        grid=(m // tm,),
        in_specs=[pl.BlockSpec((tm, da), lambda i: (i, 0)),
                  pl.BlockSpec((tm, db_), lambda i: (i, 0)),
                  pl.BlockSpec((da, d), lambda i: (0, 0)),
                  pl.BlockSpec((db_, d), lambda i: (1, 0)),
                  pl.BlockSpec((tm, d), lambda i: (i, 0)),
                  _mod_spec(mod3, d, 2, tiles_per_group)],
        out_specs=pl.BlockSpec((tm, d), lambda i: (i, 0)),
        out_shape=jax.ShapeDtypeStruct((m, d), F32),
        compiler_params=_cparams("parallel"),
        name="out_proj",
    )(a, b, w_out_bf, w_out_bf, x2, mod3)


def kernel(x_prompt, x_sample, c_prompt, c_sample, cache_k, cache_v, cache_logf, state_hgrn,
           page_table, norm_w, w_ada, b_ada, w_in, b_fox_f, lb_logits, q_norm_w, k_norm_w,
           hgrn_norm_w, w_out):
    batch, seq, d = x_prompt.shape
    db, dseq, _ = x_sample.shape
    assert dseq == 1, "the sample group is one new row per sequence"
    depth, n_phys, page_rows, fheads, hd = cache_k.shape
    _, _, hheads, dk, dv = state_hgrn.shape
    hk, dh, dfox = hheads * dk, hheads * dv, fheads * hd
    n_seg = 2 * hk + 2 * dh + 4 * dfox
    assert w_in.shape[2] == n_seg + fheads
    mp, ms = batch * seq, db * dseq
    tm = ROW_TILE

    ck = cache_k.reshape(depth, n_phys, page_rows, dfox)
    cv = cache_v.reshape(depth, n_phys, page_rows, dfox)
    clf_t = jnp.swapaxes(cache_logf, 2, 3)

    xp = x_prompt.reshape(mp, d)
    xs = x_sample.reshape(ms, d)
    r_pad = -(batch + db) % 8
    c_all = jnp.concatenate([c_prompt, c_sample, jnp.zeros((r_pad, d), F32)], axis=0)

    outs = [[] for _ in range(8)]
    for l in range(depth):
        mod = _ada_call(c_all, w_ada[l], b_ada[l])
        mod_p = mod[:batch].reshape(batch, 1, 3 * d)
        mod_s = mod[batch:batch + db].reshape(1, db, 3 * d)
        w_bf = w_in[l].astype(BF16)
        w_ff = jnp.pad(w_bf[:, n_seg:], ((0, 0), (0, LANES - fheads)))
        b_ff = jnp.pad(b_fox_f[l].astype(F32), (0, LANES - fheads)).reshape(1, LANES)
        w_out_bf = w_out[l].astype(BF16)
        proj = functools.partial(_in_proj, w_bf=w_bf, w_ff=w_ff, b_ff=b_ff, lb_logits=lb_logits,
                                 q_norm_w=q_norm_w[l], k_norm_w=k_norm_w[l], layer=l,
                                 hgrn_heads=hheads, dk=dk, fox_heads=fheads, hd=hd)

        hp = _norm_call(xp, norm_w[l], mod_p, tm, seq // tm)
        pp = proj(hp, tm=tm, seq_len=seq)
        a_p, s_p = _hgrn_prompt_call(pp, hgrn_norm_w[l], batch, seq, hheads, dk, dv)
        b_p = _fox_prompt_call(pp, batch, seq, fheads, hd)
        xp = _out_call(a_p, b_p, w_out_bf, xp, mod_p, tm, seq // tm)
        outs[0].append(pp["f_k"].reshape(batch, seq, fheads, hd))
        outs[1].append(pp["f_v"].reshape(batch, seq, fheads, hd))
        outs[2].append(pp["f_logf"].reshape(batch, seq, fheads))
        outs[3].append(s_p)

        hs = _norm_call(xs, norm_w[l], mod_s, ms, 1)
        ps = proj(hs, tm=ms, seq_len=1)
        a_s, s_s = _hgrn_step_call(ps, hgrn_norm_w[l], state_hgrn[l], hheads, dk, dv)
        b_s = _decode_call(ps, ck, cv, clf_t, page_table, l, fheads, hd)
        xs = _out_call(a_s, b_s, w_out_bf, xs, mod_s, ms, 1)
        outs[4].append(ps["f_k"].reshape(db, dseq, fheads, hd))
        outs[5].append(ps["f_v"].reshape(db, dseq, fheads, hd))
        outs[6].append(ps["f_logf"].reshape(db, dseq, fheads))
        outs[7].append(s_s)

    stk = [jnp.stack(o) for o in outs]
    return (xp.reshape(batch, seq, d), xs.reshape(db, dseq, d), *stk)
```

```python
import functools

import jax
import jax.numpy as jnp
from jax import lax
from jax.experimental import pallas as pl
from jax.experimental.pallas import tpu as pltpu

F32 = jnp.float32
BF16 = jnp.bfloat16

LANES = 128
SUBLANES = 8
VMEM_LIMIT = 56 * 1024 * 1024
RMS_EPS = 1e-6
LOG2E = 1.4426950408889634
HGRN_CHUNK = 64
ROW_TILE = 512
PAGES_PER_STEP = 8
BIAS_PAGE_TILE = 256
HEAD_PAD = 16


def _cparams(*sem):
    return pltpu.CompilerParams(dimension_semantics=sem, vmem_limit_bytes=VMEM_LIMIT)


def _dot(a, b):
    return jnp.dot(a, b, preferred_element_type=F32)


def _dot_nt(a, b):
    return lax.dot_general(a, b, (((1,), (1,)), ((), ())), preferred_element_type=F32)


def _dot_tn(a, b):
    return lax.dot_general(a, b, (((0,), (0,)), ((), ())), preferred_element_type=F32)


def _split3(x):
    hi = x.astype(BF16)
    r = x - hi.astype(F32)
    mid = r.astype(BF16)
    lo = (r - mid.astype(F32)).astype(BF16)
    return hi, mid, lo


def _dot01(m01, x):
    hi, mid, lo = _split3(x)
    return _dot(m01, hi) + _dot(m01, mid) + _dot(m01, lo)


def _dot01_nt(x, m01):
    hi, mid, lo = _split3(x)
    return _dot(hi, m01) + _dot(mid, m01) + _dot(lo, m01)


def _sigmoid(x):
    return 1.0 / (1.0 + jnp.exp(-x))


def _silu(x):
    return x * _sigmoid(x)


def _rms(x, w):
    ms = jnp.mean(x * x, axis=-1, keepdims=True)
    return x * lax.rsqrt(ms + RMS_EPS) * w


def _largest_tile(n, cap, align):
    for t in range(min(cap, n), 0, -1):
        if n % t == 0 and t % align == 0:
            return t
    return n


def _ada_kernel(c_ref, w_ref, b_ref, o_ref):
    s = _silu(c_ref[...])
    s_hi = s.astype(BF16)
    s_lo = (s - s_hi.astype(F32)).astype(BF16)
    w = w_ref[...]
    w_hi = w.astype(BF16)
    w_lo = (w - w_hi.astype(F32)).astype(BF16)
    o_ref[...] = _dot(s_hi, w_hi) + _dot(s_hi, w_lo) + _dot(s_lo, w_hi) + b_ref[...]


def _ada_call(c_all, w_ada, b_ada):
    r, d = c_all.shape
    n = w_ada.shape[1]
    tn = 512
    return pl.pallas_call(
        _ada_kernel,
        grid=(n // tn,),
        in_specs=[pl.BlockSpec((r, d), lambda j: (0, 0)),
                  pl.BlockSpec((d, tn), lambda j: (0, j)),
                  pl.BlockSpec((1, tn), lambda j: (0, j))],
        out_specs=pl.BlockSpec((r, tn), lambda j: (0, j)),
        out_shape=jax.ShapeDtypeStruct((r, n), F32),
        compiler_params=_cparams("parallel"),
        name="ada_mod",
    )(c_all, w_ada, b_ada.reshape(1, n))


def _norm_kernel(x_ref, nw_ref, shift_ref, scale_ref, h_ref):
    xn = _rms(x_ref[...], nw_ref[...])
    h_ref[...] = (xn * (1.0 + scale_ref[...]) + shift_ref[...]).astype(BF16)


def _mod_spec(mod3, d, col, tiles_per_group):
    return pl.BlockSpec((None, mod3.shape[1], d), lambda i: (i // tiles_per_group, 0, col))


def _norm_call(x2, norm_w, mod3, tm, tiles_per_group):
    m, d = x2.shape
    return pl.pallas_call(
        _norm_kernel,
        grid=(m // tm,),
        in_specs=[pl.BlockSpec((tm, d), lambda i: (i, 0)),
                  pl.BlockSpec((1, d), lambda i: (0, 0)),
                  _mod_spec(mod3, d, 0, tiles_per_group),
                  _mod_spec(mod3, d, 1, tiles_per_group)],
        out_specs=pl.BlockSpec((tm, d), lambda i: (i, 0)),
        out_shape=jax.ShapeDtypeStruct((m, d), BF16),
        compiler_params=_cparams("parallel"),
        name="norm_mod",
    )(x2, norm_w.reshape(1, d), mod3, mod3)


def _proj_plain_kernel(h_ref, w_ref, o_ref, *, act):
    z = _dot(h_ref[...], w_ref[...])
    if act:
        z = _silu(z)
    o_ref[...] = z.astype(o_ref.dtype)


def _proj_v_kernel(h_ref, w_ref, o_ref):
    o_ref[...] = _dot(h_ref[...], w_ref[...])


def _proj_vt_kernel(h_ref, w_ref, wt_ref, o_ref, ot_ref):
    h = h_ref[...]
    o_ref[...] = _dot(h, w_ref[...])
    ot_ref[...] = _dot_nt(wt_ref[...], h).astype(BF16)


def _proj_forget_kernel(h_ref, w_ref, lbl_ref, logf_ref, k_ref, *, layer):
    z = _dot(h_ref[...], w_ref[...])
    lbl = lbl_ref[...]
    e = jnp.exp(lbl - jnp.max(lbl, axis=0, keepdims=True))
    lb = jnp.sum(e[:layer + 1], axis=0, keepdims=True) / jnp.sum(e, axis=0, keepdims=True)
    logf_ref[...] = jnp.log(lb + (1.0 - lb) * _sigmoid(z))
    k_ref[...] = ((1.0 - lb) * _sigmoid(-z)).astype(BF16)


def _aug_lanes(c2, is_query, tm):
    hi, mid, lo = _split3(c2)
    hi, mid, lo = hi.astype(F32), mid.astype(F32), lo.astype(F32)
    lane = lax.broadcasted_iota(jnp.int32, (tm, LANES), 1)
    one = jnp.ones((tm, LANES), F32)
    zero = jnp.zeros((tm, LANES), F32)
    if is_query:
        parts = jnp.where(lane == 0, hi, jnp.where(lane == 1, mid, jnp.where(lane == 2, lo, zero)))
        return jnp.where((lane >= 3) & (lane < 6), one, parts).astype(BF16)
    parts = jnp.where(lane == 3, -hi, jnp.where(lane == 4, -mid, jnp.where(lane == 5, -lo, zero)))
    return jnp.where(lane < 3, one, parts).astype(BF16)


def _proj_qk_kernel(h_ref, w_ref, nw_ref, c_ref, *out_refs, heads, hd, is_query, qscale):
    z = _dot(h_ref[...], w_ref[...])
    tm = z.shape[0]
    nw = nw_ref[...]
    c = c_ref[...]
    for h in range(heads):
        n = _rms(z[:, h * hd:(h + 1) * hd], nw)
        aug = _aug_lanes(c[:, h:h + 1] * LOG2E, is_query, tm)
        if is_query:
            (aug_ref,) = out_refs
            n = n * qscale
        else:
            full_ref, aug_ref = out_refs
            full_ref[:, h * hd:(h + 1) * hd] = n
        aug_ref[:, 2 * h * hd:(2 * h + 1) * hd] = n.astype(BF16)
        aug_ref[:, (2 * h + 1) * hd:(2 * h + 2) * hd] = aug


def _proj_fgate_kernel(h_ref, w_ref, b_ref, logf_ref, c_ref, carry_ref, *, heads, tiles_per_seq):
    x = _dot(h_ref[...], w_ref[...]) + b_ref[...]
    lf = jnp.minimum(x, 0.0) - jnp.log1p(jnp.exp(-jnp.abs(x)))
    logf_ref[...] = lf[:, :heads]
    if tiles_per_seq is None:
        c_ref[...] = lf[:, :heads]
        return
    tm = lf.shape[0]

    @pl.when(pl.program_id(0) % tiles_per_seq == 0)
    def _():
        carry_ref[...] = jnp.zeros_like(carry_ref)

    row = lax.broadcasted_iota(jnp.int32, (tm, tm), 0)
    col = lax.broadcasted_iota(jnp.int32, (tm, tm), 1)
    tril = jnp.where(col <= row, 1.0, 0.0).astype(BF16)
    c = _dot01(tril, lf) + carry_ref[...]
    c_ref[...] = c[:, :heads]
    carry_ref[...] = c[tm - 1:tm, :]


def _proj_call(kernel, h, w, seg_w, seg, extra_inputs, extra_specs, out_shapes, out_specs, tm,
               scratch=(), sem="parallel", name="in_proj"):
    m, d = h.shape
    return pl.pallas_call(
        kernel,
        grid=(m // tm,),
        in_specs=[pl.BlockSpec((tm, d), lambda i: (i, 0)),
                  pl.BlockSpec((d, seg_w), lambda i: (0, seg))] + list(extra_specs),
        out_specs=out_specs,
        out_shape=out_shapes,
        scratch_shapes=list(scratch),
        compiler_params=_cparams(sem),
        name=name,
    )(h, w, *extra_inputs)


def _full_spec(shape):
    nd = len(shape)
    return pl.BlockSpec(shape, lambda i: (0,) * nd)


def _in_proj(h, w_bf, w_ff, b_ff, lb_logits, q_norm_w, k_norm_w, layer, tm, seq_len,
             hgrn_heads, dk, fox_heads, hd, w_vt=None):
    m, d = h.shape
    hk = hgrn_heads * dk
    dfox = fox_heads * hd
    row = lambda width: pl.BlockSpec((tm, width), lambda i: (i, 0))
    sds = lambda width, dt: jax.ShapeDtypeStruct((m, width), dt)
    assert hk == dfox, "segments are indexed as equal-width column blocks"
    plain = lambda seg, act, nm: _proj_call(
        functools.partial(_proj_plain_kernel, act=act), h, w_bf, hk, seg, (), (),
        sds(hk, BF16), row(hk), tm, name=nm)
    hq = plain(0, False, "proj_hq")
    a_logf, a_k = _proj_call(
        functools.partial(_proj_forget_kernel, layer=layer), h, w_bf, hk, 1,
        (lb_logits,), (_full_spec(lb_logits.shape),),
        (sds(hk, F32), sds(hk, BF16)), (row(hk), row(hk)), tm, name="proj_hf")
    hi = plain(2, False, "proj_hi")
    sg_h = plain(3, True, "proj_hg")
    tiles_per_seq = None if seq_len == 1 else seq_len // tm
    f_logf, f_c = _proj_call(
        functools.partial(_proj_fgate_kernel, heads=fox_heads, tiles_per_seq=tiles_per_seq),
        h, w_ff, LANES, 0, (b_ff,), (_full_spec(b_ff.shape),),
        (sds(fox_heads, F32), sds(fox_heads, F32)), (row(fox_heads), row(fox_heads)), tm,
        scratch=(pltpu.VMEM((1, LANES), F32),), sem="arbitrary", name="proj_ff")
    qk = lambda seg, nw, is_query, outs, specs, nm: _proj_call(
        functools.partial(_proj_qk_kernel, heads=fox_heads, hd=hd, is_query=is_query,
                          qscale=hd ** -0.5 * LOG2E),
        h, w_bf, dfox, seg, (nw.reshape(1, hd), f_c), (_full_spec((1, hd)), row(fox_heads)),
        outs, specs, tm, name=nm)
    q_aug = qk(4, q_norm_w, True, sds(2 * dfox, BF16), row(2 * dfox), "proj_fq")
    f_k, k_aug = qk(5, k_norm_w, False, (sds(dfox, F32), sds(2 * dfox, BF16)),
                    (row(dfox), row(2 * dfox)), "proj_fk")
    if w_vt is None:
        f_v = _proj_call(_proj_v_kernel, h, w_bf, dfox, 6, (), (), sds(dfox, F32), row(dfox), tm,
                         name="proj_fv")
        v_t = None
    else:
        f_v, v_t = _proj_call(
            _proj_vt_kernel, h, w_bf, dfox, 6, (w_vt,), (_full_spec(w_vt.shape),),
            (sds(dfox, F32), jax.ShapeDtypeStruct((m // tm, dfox, tm), BF16)),
            (row(dfox), pl.BlockSpec((None, dfox, tm), lambda i: (i, 0, 0))), tm, name="proj_fv")
    sg_f = plain(7, True, "proj_fg")
    return dict(hq=hq, a_logf=a_logf, a_k=a_k, hi=hi, sg_h=sg_h, f_logf=f_logf, f_c=f_c,
                q_aug=q_aug, f_k=f_k, k_aug=k_aug, f_v=f_v, v_t=v_t, sg_f=sg_f)


def _hgrn_levels(c):
    levels = []
    m = c // 2
    while m >= 1:
        levels.append(m)
        m //= 2
    return levels


def _hgrn_level_exponents(b, g, c):
    dk = b.shape[1]
    ridx = lax.broadcasted_iota(jnp.int32, (c, dk), 0)
    out = []
    for m in _hgrn_levels(c):
        pos = ridx & (2 * m - 1)
        later = pos >= m
        if 2 * m >= SUBLANES:
            b3 = b.reshape(c // (2 * m), 2 * m, dk)
            bm = jnp.broadcast_to(b3[:, m - 1:m, :], b3.shape).reshape(c, dk)
            e = jnp.where(later, b - bm, bm - b)
        elif m == 2:
            g_next = pltpu.roll(g, c - 1, 0)
            g_prev = pltpu.roll(g, 1, 0)
            e = jnp.where(pos == 0, g_next,
                          jnp.where(pos == 2, g, jnp.where(pos == 3, g + g_prev, 0.0)))
        else:
            e = jnp.where(later, g, 0.0)
        out.append((e, later))
    return out


def _hgrn_kernel(q_ref, k_ref, g_ref, v_ref, sg_ref, nw_ref, o_ref, s_ref, st_ref,
                 *, c, heads, dk, dv):
    ci = pl.program_id(1)

    @pl.when(ci == 0)
    def _():
        st_ref[...] = jnp.zeros_like(st_ref)

    row = lax.broadcasted_iota(jnp.int32, (c, c), 0)
    col = lax.broadcasted_iota(jnp.int32, (c, c), 1)
    tril = jnp.where(col <= row, 1.0, 0.0).astype(BF16)
    eye = row == col
    masks = []
    for m in _hgrn_levels(c):
        same = (row // (2 * m)) == (col // (2 * m))
        masks.append(same & ((row & (2 * m - 1)) >= m) & ((col & (2 * m - 1)) < m))
    nw = nw_ref[...]

    for h in range(heads):
        ks = slice(h * dk, (h + 1) * dk)
        vs = slice(h * dv, (h + 1) * dv)
        q = q_ref[:, ks].astype(F32)
        k = k_ref[:, ks].astype(F32)
        g = g_ref[:, ks]
        v = v_ref[:, vs]
        b = _dot01(tril, g)
        b_last = b[c - 1:c]
        a = jnp.where(eye, _dot_nt(q.astype(BF16), k.astype(BF16)), 0.0)
        for (e, later), mask in zip(_hgrn_level_exponents(b, g, c), masks):
            x = (jnp.where(later, q, k) * jnp.exp(e)).astype(BF16)
            a = a + jnp.where(mask, _dot_nt(x, x), 0.0)
        st = st_ref[h]
        o = _dot_nt((q * jnp.exp(b)).astype(BF16), st.astype(BF16)) + _dot(a.astype(BF16), v)
        kd = (k * jnp.exp(b_last - b)).astype(BF16)
        st_ref[h] = st * jnp.exp(b_last) + _dot_tn(v, kd)
        o_ref[:, vs] = (_rms(o, nw) * sg_ref[:, vs].astype(F32)).astype(BF16)

    @pl.when(ci == pl.num_programs(1) - 1)
    def _():
        for h in range(heads):
            s_ref[h] = st_ref[h].T


def _hgrn_prompt_call(p, hgrn_norm_w, batch, seq, heads, dk, dv):
    c = HGRN_CHUNK
    nc = seq // c
    blk = lambda width: pl.BlockSpec((c, heads * width), lambda b, i: (b * nc + i, 0))
    return pl.pallas_call(
        functools.partial(_hgrn_kernel, c=c, heads=heads, dk=dk, dv=dv),
        grid=(batch, nc),
        in_specs=[blk(dk), blk(dk), blk(dk), blk(dv), blk(dv),
                  pl.BlockSpec((1, dv), lambda b, i: (0, 0))],
        out_specs=(blk(dv), pl.BlockSpec((None, heads, dk, dv), lambda b, i: (b, 0, 0, 0))),
        out_shape=(jax.ShapeDtypeStruct((batch * seq, heads * dv), BF16),
                   jax.ShapeDtypeStruct((batch, heads, dk, dv), F32)),
        scratch_shapes=[pltpu.VMEM((heads, dv, dk), F32)],
        compiler_params=_cparams("parallel", "arbitrary"),
        name="hgrn_prompt",
    )(p["hq"], p["a_k"], p["a_logf"], p["hi"], p["sg_h"], hgrn_norm_w.reshape(1, dv))


def _hgrn_step_kernel(q_ref, k_ref, g_ref, v_ref, sg_ref, nw_ref, s0_ref, o_ref, s_ref, *, heads):
    pad = lambda x: jnp.concatenate([x, jnp.zeros((LANES - heads, x.shape[1]), F32)], axis=0)
    qt = pad(q_ref[...].astype(F32)).T
    kt = pad(k_ref[...].astype(F32)).T
    ft = pad(jnp.exp(g_ref[...])).T
    v = v_ref[...].astype(F32)
    sg = sg_ref[...].astype(F32)
    rows = []
    for h in range(heads):
        s = ft[:, h:h + 1] * s0_ref[h] + kt[:, h:h + 1] * v[h:h + 1, :]
        s_ref[h] = s
        o = jnp.sum(qt[:, h:h + 1] * s, axis=0, keepdims=True)
        rows.append(_rms(o, nw_ref[...]) * sg[h:h + 1, :])
    o_ref[...] = jnp.concatenate(rows, axis=0).astype(BF16)


def _hgrn_step_call(p, hgrn_norm_w, state, heads, dk, dv):
    db = state.shape[0]
    r3 = lambda x, w: x.reshape(db, heads, w)
    blk = lambda w: pl.BlockSpec((None, heads, w), lambda b: (b, 0, 0))
    st_blk = pl.BlockSpec((None, heads, dk, dv), lambda b: (b, 0, 0, 0))
    o, s = pl.pallas_call(
        functools.partial(_hgrn_step_kernel, heads=heads),
        grid=(db,),
        in_specs=[blk(dk), blk(dk), blk(dk), blk(dv), blk(dv),
                  pl.BlockSpec((1, dv), lambda b: (0, 0)), st_blk],
        out_specs=(blk(dv), st_blk),
        out_shape=(jax.ShapeDtypeStruct((db, heads, dv), BF16),
                   jax.ShapeDtypeStruct(state.shape, F32)),
        compiler_params=_cparams("parallel"),
        name="hgrn_step",
    )(r3(p["hq"], dk), r3(p["a_k"], dk), r3(p["a_logf"], dk), r3(p["hi"], dv), r3(p["sg_h"], dv),
      hgrn_norm_w.reshape(1, dv), state)
    return o.reshape(db, heads * dv), s


def _fox_kernel(q_ref, k_ref, vt_ref, sg_ref, o_ref, s_ref, m_ref, l_ref, acc_ref, *, t):
    qi = pl.program_id(2)
    q = q_ref[...]
    m_ref[...] = jnp.full_like(m_ref, -jnp.inf)
    l_ref[...] = jnp.zeros_like(l_ref)
    acc_ref[...] = jnp.zeros_like(acc_ref)

    def scores(j):
        return _dot_nt(k_ref[pl.ds(pl.multiple_of(j * t, t), t), :], q)

    def update(j, s):
        m_prev = m_ref[...]
        m_new = jnp.maximum(m_prev, jnp.max(s, axis=0, keepdims=True))
        alpha = jnp.exp2(m_prev - m_new)
        p = jnp.exp2(s - m_new)
        l_ref[...] = alpha * l_ref[...] + jnp.sum(p, axis=0, keepdims=True)
        acc_ref[...] = alpha * acc_ref[...] + _dot(vt_ref[j], p.astype(BF16))
        m_ref[...] = m_new

    s_ref[...] = scores(0)

    def body(j, carry):
        s = s_ref[...]
        s_next = scores(j + 1)
        update(j, s)
        s_ref[...] = s_next
        return carry

    lax.fori_loop(0, qi, body, 0)
    key = lax.broadcasted_iota(jnp.int32, (t, t), 0)
    qry = lax.broadcasted_iota(jnp.int32, (t, t), 1)
    update(qi, jnp.where(key <= qry, s_ref[...], -jnp.inf))
    o = (acc_ref[...] / l_ref[...]).T
    o_ref[...] = (o * sg_ref[...].astype(F32)).astype(BF16)


def _fox_prompt_call(p, batch, seq, heads, hd, t):
    nq = seq // t
    return pl.pallas_call(
        functools.partial(_fox_kernel, t=t),
        grid=(batch, heads, nq),
        in_specs=[pl.BlockSpec((t, 2 * hd), lambda b, h, i: (b * nq + i, h)),
                  pl.BlockSpec((seq, 2 * hd), lambda b, h, i: (b, h)),
                  pl.BlockSpec((nq, hd, t), lambda b, h, i: (b, h, 0)),
                  pl.BlockSpec((t, hd), lambda b, h, i: (b * nq + i, h))],
        out_specs=pl.BlockSpec((t, hd), lambda b, h, i: (b * nq + i, h)),
        out_shape=jax.ShapeDtypeStruct((batch * seq, heads * hd), BF16),
        scratch_shapes=[pltpu.VMEM((t, t), F32), pltpu.VMEM((1, t), F32), pltpu.VMEM((1, t), F32),
                        pltpu.VMEM((hd, t), F32)],
        compiler_params=_cparams("parallel", "parallel", "arbitrary"),
        name="fox_prompt",
    )(p["q_aug"], p["k_aug"], p["v_t"], p["sg_f"])


def _page_bias_kernel(x_ref, u_ref, rho_ref, tot_ref, *, heads):
    x = x_ref[...]
    rho_ref[...] = _dot01_nt(x, u_ref[...])
    n = x.shape[1]
    t = x[:, :LANES]
    for i in range(1, n // LANES):
        t = t + x[:, i * LANES:(i + 1) * LANES]
    shift = heads
    while shift < LANES:
        t = t + pltpu.roll(t, shift, 1)
        shift *= 2
    tot_ref[...] = jnp.concatenate([t] * (n // LANES), axis=1)


def _page_bias_call(lf_flat, heads):
    n_pages, n = lf_flat.shape
    rows = n // heads
    assert LANES % heads == 0 and n % LANES == 0
    idx = jnp.arange(n, dtype=jnp.int32)
    later = (idx[:, None] // heads) > (idx[None, :] // heads)
    same_head = (idx[:, None] % heads) == (idx[None, :] % heads)
    u = (later & same_head).astype(BF16)
    pb = _largest_tile(n_pages, BIAS_PAGE_TILE, SUBLANES)
    blk = pl.BlockSpec((pb, n), lambda i: (i, 0))
    del rows
    return pl.pallas_call(
        functools.partial(_page_bias_kernel, heads=heads),
        grid=(n_pages // pb,),
        in_specs=[blk, pl.BlockSpec((n, n), lambda i: (0, 0))],
        out_specs=(blk, blk),
        out_shape=(jax.ShapeDtypeStruct((n_pages, n), F32),) * 2,
        compiler_params=_cparams("parallel"),
        name="page_bias",
    )(lf_flat, u)


def _decode_kernel(pt_ref, q_ref, knew_ref, vnew_ref, cn_ref, sg_ref, *refs, pps, heads):
    k_refs = refs[:pps]
    v_refs = refs[pps:2 * pps]
    rho_refs = refs[2 * pps:3 * pps]
    tot_refs = refs[3 * pps:4 * pps]
    o_ref, m_ref, l_ref, run_ref, acc_ref = refs[4 * pps:]
    j = pl.program_id(1)
    rows, _, hd = k_refs[0].shape
    n = rows * heads

    @pl.when(j == 0)
    def _():
        m_ref[...] = jnp.full_like(m_ref, -jnp.inf)
        l_ref[...] = jnp.zeros_like(l_ref)
        run_ref[...] = jnp.zeros_like(run_ref)
        acc_ref[...] = jnp.zeros_like(acc_ref)

    q = q_ref[...]
    sub = lax.broadcasted_iota(jnp.int32, (HEAD_PAD, n), 0)
    lane = lax.broadcasted_iota(jnp.int32, (HEAD_PAD, n), 1)
    own = (lane % heads) == (sub % heads)
    cn = cn_ref[...]
    run = run_ref[...]
    scores = []
    for p in range(pps):
        kb = k_refs[p][...].reshape(n, hd).astype(BF16)
        bias = (rho_refs[p][...] + run + cn) * LOG2E
        scores.append(jnp.where(own, _dot_nt(q, kb) + bias, -jnp.inf))
        run = run + tot_refs[p][...]
    run_ref[...] = run

    m_prev = m_ref[:, :1]
    m_new = m_prev
    for s in scores:
        m_new = jnp.maximum(m_new, jnp.max(s, axis=-1, keepdims=True))
    alpha = jnp.exp2(m_prev - m_new)
    l_new = alpha * l_ref[:, :1]
    acc = alpha * acc_ref[...]
    for p in range(pps):
        pr = jnp.exp2(scores[p] - m_new)
        l_new = l_new + jnp.sum(pr, axis=-1, keepdims=True)
        acc = acc + _dot(pr.astype(BF16), v_refs[p][...].reshape(n, hd).astype(BF16))
    m_ref[...] = jnp.broadcast_to(m_new, m_ref.shape)
    l_ref[...] = jnp.broadcast_to(l_new, l_ref.shape)
    acc_ref[...] = acc

    @pl.when(j == pl.num_programs(1) - 1)
    def _():
        s_new = jnp.sum(q.astype(F32) * knew_ref[...].astype(F32), axis=-1, keepdims=True)
        m_fin = jnp.maximum(m_new, s_new)
        a_fin = jnp.exp2(m_new - m_fin)
        p_new = jnp.exp2(s_new - m_fin)
        l_fin = a_fin * l_new + p_new
        out = (a_fin * acc + p_new * vnew_ref[...]) / l_fin
        o_ref[...] = (out * sg_ref[...].astype(F32)).astype(BF16)


def _decode_call(p, cache_k, cache_v, rho, tot, page_table, layer, heads, hd):
    db, n_pg = page_table.shape
    rows = cache_k.shape[2]
    n = rows * heads
    pps = _largest_tile(n_pg, PAGES_PER_STEP, 1)
    d = heads * hd
    padh = lambda x: jnp.pad(x, ((0, 0), (0, HEAD_PAD - heads), (0, 0)))
    head_rows = lambda x: padh(x.reshape(db, heads, hd))
    q16 = padh(p["q_aug"].reshape(db, heads, 2 * hd)[:, :, :hd])
    k16 = padh(p["k_aug"].reshape(db, heads, 2 * hd)[:, :, :hd])
    cn = jnp.tile(p["f_logf"], (1, rows)).reshape(db, 1, n)

    seq_blk = lambda shape: pl.BlockSpec((None,) + shape, lambda b, j, pt: (b, 0, 0))
    page_of = lambda b, j, pt, slot: pt[b, n_pg - 1 - (j * pps + slot)]

    def cache_blk(slot):
        return pl.BlockSpec((None, None, rows, heads, hd),
                            lambda b, j, pt: (layer, page_of(b, j, pt, slot), 0, 0, 0))

    def bias_blk(slot):
        return pl.BlockSpec((None, 1, n), lambda b, j, pt: (page_of(b, j, pt, slot), 0, 0))

    in_specs = ([seq_blk((HEAD_PAD, hd)), seq_blk((HEAD_PAD, hd)), seq_blk((HEAD_PAD, hd)),
                 seq_blk((1, n)), seq_blk((HEAD_PAD, hd))]
                + [cache_blk(s) for s in range(pps)] + [cache_blk(s) for s in range(pps)]
                + [bias_blk(s) for s in range(pps)] + [bias_blk(s) for s in range(pps)])
    grid_spec = pltpu.PrefetchScalarGridSpec(
        num_scalar_prefetch=1,
        grid=(db, n_pg // pps),
        in_specs=in_specs,
        out_specs=seq_blk((HEAD_PAD, hd)),
        scratch_shapes=[pltpu.VMEM((HEAD_PAD, LANES), F32), pltpu.VMEM((HEAD_PAD, LANES), F32),
                        pltpu.VMEM((1, n), F32), pltpu.VMEM((HEAD_PAD, hd), F32)])
    out = pl.pallas_call(
        functools.partial(_decode_kernel, pps=pps, heads=heads),
        grid_spec=grid_spec,
        out_shape=jax.ShapeDtypeStruct((db, HEAD_PAD, hd), BF16),
        compiler_params=_cparams("parallel", "arbitrary"),
        name="fox_decode",
    )(page_table, q16, k16, head_rows(p["f_v"]), cn, head_rows(p["sg_f"]),
      *([cache_k] * pps), *([cache_v] * pps), *([rho] * pps), *([tot] * pps))
    return out[:, :heads, :].reshape(db, d)


def _out_kernel(a_ref, b_ref, wa_ref, wb_ref, x_ref, gate_ref, y_ref):
    out = _dot(a_ref[...], wa_ref[...]) + _dot(b_ref[...], wb_ref[...])
    y_ref[...] = x_ref[...] + gate_ref[...] * out


def _out_call(a, b, w_out_bf, x2, mod3, tm, tiles_per_group):
    m, d = x2.shape
    da, db_ = a.shape[1], b.shape[1]
    assert da == db_, "w_out is read as two equal row blocks"
    return pl.pallas_call(
        _out_kernel,
        grid=(m // tm,),
        in_specs=[pl.BlockSpec((tm, da), lambda i: (i, 0)),
                  pl.BlockSpec((tm, db_), lambda i: (i, 0)),
                  pl.BlockSpec((da, d), lambda i: (0, 0)),
                  pl.BlockSpec((db_, d), lambda i: (1, 0)),
                  pl.BlockSpec((tm, d), lambda i: (i, 0)),
                  _mod_spec(mod3, d, 2, tiles_per_group)],
        out_specs=pl.BlockSpec((tm, d), lambda i: (i, 0)),
        out_shape=jax.ShapeDtypeStruct((m, d), F32),
        compiler_params=_cparams("parallel"),
        name="out_proj",
    )(a, b, w_out_bf, w_out_bf, x2, mod3)


def kernel(x_prompt, x_sample, c_prompt, c_sample, cache_k, cache_v, cache_logf, state_hgrn,
           page_table, norm_w, w_ada, b_ada, w_in, b_fox_f, lb_logits, q_norm_w, k_norm_w,
           hgrn_norm_w, w_out):
    batch, seq, d = x_prompt.shape
    db, dseq, _ = x_sample.shape
    assert dseq == 1, "the sample group is one new row per sequence"
    depth, n_phys, page_rows, fheads, hd = cache_k.shape
    _, _, hheads, dk, dv = state_hgrn.shape
    hk, dh, dfox = hheads * dk, hheads * dv, fheads * hd
    n_seg = 2 * hk + 2 * dh + 4 * dfox
    assert w_in.shape[2] == n_seg + fheads
    mp, ms = batch * seq, db * dseq
    tm = _largest_tile(seq, ROW_TILE, LANES)

    xp = x_prompt.reshape(mp, d)
    xs = x_sample.reshape(ms, d)
    r_pad = -(batch + db) % SUBLANES
    c_all = jnp.concatenate([c_prompt, c_sample, jnp.zeros((r_pad, d), F32)], axis=0)

    outs = [[] for _ in range(8)]
    for l in range(depth):
        mod = _ada_call(c_all, w_ada[l], b_ada[l])
        mod_p = mod[:batch].reshape(batch, 1, 3 * d)
        mod_s = mod[batch:batch + db].reshape(1, db, 3 * d)
        w_bf = w_in[l].astype(BF16)
        w_ff = jnp.pad(w_bf[:, n_seg:], ((0, 0), (0, LANES - fheads)))
        b_ff = jnp.pad(b_fox_f[l].astype(F32), (0, LANES - fheads)).reshape(1, LANES)
        w_vt = w_bf[:, 2 * hk + 2 * dh + 2 * dfox:2 * hk + 2 * dh + 3 * dfox].T
        w_out_bf = w_out[l].astype(BF16)
        proj = functools.partial(_in_proj, w_bf=w_bf, w_ff=w_ff, b_ff=b_ff, lb_logits=lb_logits,
                                 q_norm_w=q_norm_w[l], k_norm_w=k_norm_w[l], layer=l,
                                 hgrn_heads=hheads, dk=dk, fox_heads=fheads, hd=hd)

        hp = _norm_call(xp, norm_w[l], mod_p, tm, seq // tm)
        pp = proj(hp, tm=tm, seq_len=seq, w_vt=w_vt)
        a_p, s_p = _hgrn_prompt_call(pp, hgrn_norm_w[l], batch, seq, hheads, dk, dv)
        b_p = _fox_prompt_call(pp, batch, seq, fheads, hd, tm)
        xp = _out_call(a_p, b_p, w_out_bf, xp, mod_p, tm, seq // tm)
        outs[0].append(pp["f_k"].reshape(batch, seq, fheads, hd))
        outs[1].append(pp["f_v"].reshape(batch, seq, fheads, hd))
        outs[2].append(pp["f_logf"].reshape(batch, seq, fheads))
        outs[3].append(s_p)

        hs = _norm_call(xs, norm_w[l], mod_s, ms, 1)
        ps = proj(hs, tm=ms, seq_len=1)
        a_s, s_s = _hgrn_step_call(ps, hgrn_norm_w[l], state_hgrn[l], hheads, dk, dv)
        rho, tot = _page_bias_call(cache_logf[l].reshape(n_phys, page_rows * fheads), fheads)
        rho = rho.reshape(n_phys, 1, page_rows * fheads)
        tot = tot.reshape(n_phys, 1, page_rows * fheads)
        b_s = _decode_call(ps, cache_k, cache_v, rho, tot, page_table, l, fheads, hd)
        xs = _out_call(a_s, b_s, w_out_bf, xs, mod_s, ms, 1)
        outs[4].append(ps["f_k"].reshape(db, dseq, fheads, hd))
        outs[5].append(ps["f_v"].reshape(db, dseq, fheads, hd))
        outs[6].append(ps["f_logf"].reshape(db, dseq, fheads))
        outs[7].append(s_s)

    stk = [jnp.stack(o) for o in outs]
    return (xp.reshape(batch, seq, d), xs.reshape(db, dseq, d), *stk)
```

```python
import functools
from typing import NamedTuple

import jax
import jax.numpy as jnp
from jax import lax
from jax.experimental import pallas as pl
from jax.experimental.pallas import tpu as pltpu

F32 = jnp.float32
BF16 = jnp.bfloat16

LANES = 128
SUBLANES = 8
VMEM_LIMIT = 56 * 1024 * 1024
RMS_EPS = 1e-6
LOG2E = 1.4426950408889634
HGRN_CHUNK = 64
ROW_TILE = 512
PAGES_PER_STEP = 8
BIAS_PAGE_TILE = 256
HEAD_PAD = 16


def _cparams(*sem):
    return pltpu.CompilerParams(dimension_semantics=sem, vmem_limit_bytes=VMEM_LIMIT)


def _dot(a, b):
    return jnp.dot(a, b, preferred_element_type=F32)


def _dot_nt(a, b):
    return lax.dot_general(a, b, (((1,), (1,)), ((), ())), preferred_element_type=F32)


def _dot_tn(a, b):
    return lax.dot_general(a, b, (((0,), (0,)), ((), ())), preferred_element_type=F32)


def _split3(x):
    hi = x.astype(BF16)
    r = x - hi.astype(F32)
    mid = r.astype(BF16)
    lo = (r - mid.astype(F32)).astype(BF16)
    return hi, mid, lo


def _dot01(m01, x):
    hi, mid, lo = _split3(x)
    return _dot(m01, hi) + _dot(m01, mid) + _dot(m01, lo)


def _dot01_nt(x, m01):
    hi, mid, lo = _split3(x)
    return _dot(hi, m01) + _dot(mid, m01) + _dot(lo, m01)


def _sigmoid(x):
    return 1.0 / (1.0 + jnp.exp(-x))


def _silu(x):
    return x * _sigmoid(x)


def _rms(x, w):
    ms = jnp.mean(x * x, axis=-1, keepdims=True)
    return x * lax.rsqrt(ms + RMS_EPS) * w


def _largest_tile(n, cap, align):
    for t in range(min(cap, n), 0, -1):
        if n % t == 0 and t % align == 0:
            return t
    return n


def _ada_kernel(c_ref, w_ref, b_ref, o_ref):
    s = _silu(c_ref[...])
    s_hi = s.astype(BF16)
    s_lo = (s - s_hi.astype(F32)).astype(BF16)
    w = w_ref[...]
    w_hi = w.astype(BF16)
    w_lo = (w - w_hi.astype(F32)).astype(BF16)
    o_ref[...] = _dot(s_hi, w_hi) + _dot(s_hi, w_lo) + _dot(s_lo, w_hi) + b_ref[...]


def _ada_call(c_all, w_ada, b_ada):
    r, d = c_all.shape
    n = w_ada.shape[1]
    tn = 512
    return pl.pallas_call(
        _ada_kernel,
        grid=(n // tn,),
        in_specs=[pl.BlockSpec((r, d), lambda j: (0, 0)),
                  pl.BlockSpec((d, tn), lambda j: (0, j)),
                  pl.BlockSpec((1, tn), lambda j: (0, j))],
        out_specs=pl.BlockSpec((r, tn), lambda j: (0, j)),
        out_shape=jax.ShapeDtypeStruct((r, n), F32),
        compiler_params=_cparams("parallel"),
        name="ada_mod",
    )(c_all, w_ada, b_ada.reshape(1, n))


def _norm_kernel(x_ref, nw_ref, shift_ref, scale_ref, h_ref):
    xn = _rms(x_ref[...], nw_ref[...])
    h_ref[...] = (xn * (1.0 + scale_ref[...]) + shift_ref[...]).astype(BF16)


def _mod_spec(mod3, d, col, tiles_per_group):
    return pl.BlockSpec((None, mod3.shape[1], d), lambda i: (i // tiles_per_group, 0, col))


def _norm_call(x2, norm_w, mod3, tm, tiles_per_group):
    m, d = x2.shape
    return pl.pallas_call(
        _norm_kernel,
        grid=(m // tm,),
        in_specs=[pl.BlockSpec((tm, d), lambda i: (i, 0)),
                  pl.BlockSpec((1, d), lambda i: (0, 0)),
                  _mod_spec(mod3, d, 0, tiles_per_group),
                  _mod_spec(mod3, d, 1, tiles_per_group)],
        out_specs=pl.BlockSpec((tm, d), lambda i: (i, 0)),
        out_shape=jax.ShapeDtypeStruct((m, d), BF16),
        compiler_params=_cparams("parallel"),
        name="norm_mod",
    )(x2, norm_w.reshape(1, d), mod3, mod3)


def _proj_plain_kernel(h_ref, w_ref, o_ref, *, act):
    z = _dot(h_ref[...], w_ref[...])
    if act:
        z = _silu(z)
    o_ref[...] = z.astype(o_ref.dtype)


def _proj_v_kernel(h_ref, w_ref, o_ref):
    o_ref[...] = _dot(h_ref[...], w_ref[...])


def _proj_vt_kernel(h_ref, w_ref, wt_ref, o_ref, ot_ref):
    h = h_ref[...]
    o_ref[...] = _dot(h, w_ref[...])
    ot_ref[...] = _dot_nt(wt_ref[...], h).astype(BF16)


def _proj_forget_kernel(h_ref, w_ref, lbl_ref, logf_ref, k_ref, *, layer):
    z = _dot(h_ref[...], w_ref[...])
    lbl = lbl_ref[...]
    e = jnp.exp(lbl - jnp.max(lbl, axis=0, keepdims=True))
    lb = jnp.sum(e[:layer + 1], axis=0, keepdims=True) / jnp.sum(e, axis=0, keepdims=True)
    logf_ref[...] = jnp.log(lb + (1.0 - lb) * _sigmoid(z))
    k_ref[...] = ((1.0 - lb) * _sigmoid(-z)).astype(BF16)


def _aug_lanes(c2, is_query, tm):
    hi, mid, lo = _split3(c2)
    hi, mid, lo = hi.astype(F32), mid.astype(F32), lo.astype(F32)
    lane = lax.broadcasted_iota(jnp.int32, (tm, LANES), 1)
    one = jnp.ones((tm, LANES), F32)
    zero = jnp.zeros((tm, LANES), F32)
    if is_query:
        parts = jnp.where(lane == 0, hi, jnp.where(lane == 1, mid, jnp.where(lane == 2, lo, zero)))
        return jnp.where((lane >= 3) & (lane < 6), one, parts).astype(BF16)
    parts = jnp.where(lane == 3, -hi, jnp.where(lane == 4, -mid, jnp.where(lane == 5, -lo, zero)))
    return jnp.where(lane < 3, one, parts).astype(BF16)


def _proj_qk_kernel(h_ref, w_ref, nw_ref, c_ref, *out_refs, heads, hd, is_query, qscale):
    z = _dot(h_ref[...], w_ref[...])
    tm = z.shape[0]
    nw = nw_ref[...]
    c = c_ref[...]
    for h in range(heads):
        n = _rms(z[:, h * hd:(h + 1) * hd], nw)
        aug = _aug_lanes(c[:, h:h + 1] * LOG2E, is_query, tm)
        if is_query:
            (aug_ref,) = out_refs
            n = n * qscale
        else:
            full_ref, aug_ref = out_refs
            full_ref[:, h * hd:(h + 1) * hd] = n
        aug_ref[:, 2 * h * hd:(2 * h + 1) * hd] = n.astype(BF16)
        aug_ref[:, (2 * h + 1) * hd:(2 * h + 2) * hd] = aug


def _proj_fgate_kernel(h_ref, w_ref, b_ref, logf_ref, c_ref, carry_ref, *, heads, tiles_per_seq):
    x = _dot(h_ref[...], w_ref[...]) + b_ref[...]
    lf = jnp.minimum(x, 0.0) - jnp.log1p(jnp.exp(-jnp.abs(x)))
    logf_ref[...] = lf[:, :heads]
    if tiles_per_seq is None:
        c_ref[...] = lf[:, :heads]
        return
    tm = lf.shape[0]

    @pl.when(pl.program_id(0) % tiles_per_seq == 0)
    def _():
        carry_ref[...] = jnp.zeros_like(carry_ref)

    row = lax.broadcasted_iota(jnp.int32, (tm, tm), 0)
    col = lax.broadcasted_iota(jnp.int32, (tm, tm), 1)
    tril = jnp.where(col <= row, 1.0, 0.0).astype(BF16)
    c = _dot01(tril, lf) + carry_ref[...]
    c_ref[...] = c[:, :heads]
    carry_ref[...] = c[tm - 1:tm, :]


def _proj_call(kernel, h, w, seg_w, seg, extra_inputs, extra_specs, out_shapes, out_specs, tm,
               scratch=(), sem="parallel", name="in_proj"):
    m, d = h.shape
    return pl.pallas_call(
        kernel,
        grid=(m // tm,),
        in_specs=[pl.BlockSpec((tm, d), lambda i: (i, 0)),
                  pl.BlockSpec((d, seg_w), lambda i: (0, seg))] + list(extra_specs),
        out_specs=out_specs,
        out_shape=out_shapes,
        scratch_shapes=list(scratch),
        compiler_params=_cparams(sem),
        name=name,
    )(h, w, *extra_inputs)


def _full_spec(shape):
    nd = len(shape)
    return pl.BlockSpec(shape, lambda i: (0,) * nd)


def _in_proj(h, w_bf, w_ff, b_ff, lb_logits, q_norm_w, k_norm_w, layer, tm, seq_len,
             hgrn_heads, dk, fox_heads, hd, w_vt=None):
    m, d = h.shape
    hk = hgrn_heads * dk
    dfox = fox_heads * hd
    row = lambda width: pl.BlockSpec((tm, width), lambda i: (i, 0))
    sds = lambda width, dt: jax.ShapeDtypeStruct((m, width), dt)
    assert hk == dfox, "segments are indexed as equal-width column blocks"
    plain = lambda seg, act, nm: _proj_call(
        functools.partial(_proj_plain_kernel, act=act), h, w_bf, hk, seg, (), (),
        sds(hk, BF16), row(hk), tm, name=nm)
    hq = plain(0, False, "proj_hq")
    a_logf, a_k = _proj_call(
        functools.partial(_proj_forget_kernel, layer=layer), h, w_bf, hk, 1,
        (lb_logits,), (_full_spec(lb_logits.shape),),
        (sds(hk, F32), sds(hk, BF16)), (row(hk), row(hk)), tm, name="proj_hf")
    hi = plain(2, False, "proj_hi")
    sg_h = plain(3, True, "proj_hg")
    tiles_per_seq = None if seq_len == 1 else seq_len // tm
    f_logf, f_c = _proj_call(
        functools.partial(_proj_fgate_kernel, heads=fox_heads, tiles_per_seq=tiles_per_seq),
        h, w_ff, LANES, 0, (b_ff,), (_full_spec(b_ff.shape),),
        (sds(fox_heads, F32), sds(fox_heads, F32)), (row(fox_heads), row(fox_heads)), tm,
        scratch=(pltpu.VMEM((1, LANES), F32),), sem="arbitrary", name="proj_ff")
    qk = lambda seg, nw, is_query, outs, specs, nm: _proj_call(
        functools.partial(_proj_qk_kernel, heads=fox_heads, hd=hd, is_query=is_query,
                          qscale=hd ** -0.5 * LOG2E),
        h, w_bf, dfox, seg, (nw.reshape(1, hd), f_c), (_full_spec((1, hd)), row(fox_heads)),
        outs, specs, tm, name=nm)
    q_aug = qk(4, q_norm_w, True, sds(2 * dfox, BF16), row(2 * dfox), "proj_fq")
    f_k, k_aug = qk(5, k_norm_w, False, (sds(dfox, F32), sds(2 * dfox, BF16)),
                    (row(dfox), row(2 * dfox)), "proj_fk")
    if w_vt is None:
        f_v = _proj_call(_proj_v_kernel, h, w_bf, dfox, 6, (), (), sds(dfox, F32), row(dfox), tm,
                         name="proj_fv")
        v_t = None
    else:
        f_v, v_t = _proj_call(
            _proj_vt_kernel, h, w_bf, dfox, 6, (w_vt,), (_full_spec(w_vt.shape),),
            (sds(dfox, F32), jax.ShapeDtypeStruct((m // tm, dfox, tm), BF16)),
            (row(dfox), pl.BlockSpec((None, dfox, tm), lambda i: (i, 0, 0))), tm, name="proj_fv")
    sg_f = plain(7, True, "proj_fg")
    return dict(hq=hq, a_logf=a_logf, a_k=a_k, hi=hi, sg_h=sg_h, f_logf=f_logf, f_c=f_c,
                q_aug=q_aug, f_k=f_k, k_aug=k_aug, f_v=f_v, v_t=v_t, sg_f=sg_f)


def _hgrn_levels(c):
    levels = []
    m = c // 2
    while m >= 1:
        levels.append(m)
        m //= 2
    return levels


def _hgrn_level_exponents(b, g, c):
    dk = b.shape[1]
    ridx = lax.broadcasted_iota(jnp.int32, (c, dk), 0)
    out = []
    for m in _hgrn_levels(c):
        pos = ridx & (2 * m - 1)
        later = pos >= m
        if 2 * m >= SUBLANES:
            b3 = b.reshape(c // (2 * m), 2 * m, dk)
            bm = jnp.broadcast_to(b3[:, m - 1:m, :], b3.shape).reshape(c, dk)
            e = jnp.where(later, b - bm, bm - b)
        elif m == 2:
            g_next = pltpu.roll(g, c - 1, 0)
            g_prev = pltpu.roll(g, 1, 0)
            e = jnp.where(pos == 0, g_next,
                          jnp.where(pos == 2, g, jnp.where(pos == 3, g + g_prev, 0.0)))
        else:
            e = jnp.where(later, g, 0.0)
        out.append((e, later))
    return out


def _hgrn_kernel(pt_ref, q_ref, k_ref, g_ref, v_ref, sg_ref, nw_ref, *rest, c, heads, dk, dv, dec):
    ci = pl.program_id(1)
    if dec is None:
        o_ref, s_ref, st_ref = rest
    else:
        n_in = _dec_num_inputs(dec)
        (o_ref, s_ref, do_ref), st_ref = rest[n_in:n_in + 3], rest[n_in + 3]
        lin = pl.program_id(0) * pl.num_programs(1) + ci
        dec_args = (dec, pt_ref, lin, rest[:n_in], do_ref, rest[n_in + 4:])

    @pl.when(ci == 0)
    def _():
        st_ref[...] = jnp.zeros_like(st_ref)

    dec_state = None if dec is None else _dec_main(*dec_args)

    row = lax.broadcasted_iota(jnp.int32, (c, c), 0)
    col = lax.broadcasted_iota(jnp.int32, (c, c), 1)
    tril = jnp.where(col <= row, 1.0, 0.0).astype(BF16)
    eye = row == col
    masks = []
    for m in _hgrn_levels(c):
        same = (row // (2 * m)) == (col // (2 * m))
        masks.append(same & ((row & (2 * m - 1)) >= m) & ((col & (2 * m - 1)) < m))
    nw = nw_ref[...]

    for h in range(heads):
        ks = slice(h * dk, (h + 1) * dk)
        vs = slice(h * dv, (h + 1) * dv)
        q = q_ref[:, ks].astype(F32)
        k = k_ref[:, ks].astype(F32)
        g = g_ref[:, ks]
        v = v_ref[:, vs]
        b = _dot01(tril, g)
        b_last = b[c - 1:c]
        a = jnp.where(eye, _dot_nt(q.astype(BF16), k.astype(BF16)), 0.0)
        for (e, later), mask in zip(_hgrn_level_exponents(b, g, c), masks):
            x = (jnp.where(later, q, k) * jnp.exp(e)).astype(BF16)
            a = a + jnp.where(mask, _dot_nt(x, x), 0.0)
        st = st_ref[h]
        o = _dot_nt((q * jnp.exp(b)).astype(BF16), st.astype(BF16)) + _dot(a.astype(BF16), v)
        kd = (k * jnp.exp(b_last - b)).astype(BF16)
        st_ref[h] = st * jnp.exp(b_last) + _dot_tn(v, kd)
        o_ref[:, vs] = (_rms(o, nw) * sg_ref[:, vs].astype(F32)).astype(BF16)

    @pl.when(ci == pl.num_programs(1) - 1)
    def _():
        for h in range(heads):
            s_ref[h] = st_ref[h].T

    if dec is not None:
        _dec_finish(*dec_args, dec_state)


def _hgrn_steps(batch, seq):
    return batch * (seq // HGRN_CHUNK)


def _hgrn_prompt_call(p, hgrn_norm_w, batch, seq, heads, dk, dv, page_table, dec=None, dec_ops=()):
    c = HGRN_CHUNK
    nc = seq // c
    blk = lambda width: pl.BlockSpec((c, heads * width), lambda b, i, pt: (b * nc + i, 0))
    in_specs = [blk(dk), blk(dk), blk(dk), blk(dv), blk(dv),
                pl.BlockSpec((1, dv), lambda b, i, pt: (0, 0))]
    out_specs = [blk(dv), pl.BlockSpec((None, heads, dk, dv), lambda b, i, pt: (b, 0, 0, 0))]
    out_shape = [jax.ShapeDtypeStruct((batch * seq, heads * dv), BF16),
                 jax.ShapeDtypeStruct((batch, heads, dk, dv), F32)]
    scratch = [pltpu.VMEM((heads, dv, dk), F32)]
    if dec is not None:
        d_in, d_out, d_shape, d_scratch = _dec_specs(dec, lambda b, i: b * nc + i)
        in_specs += d_in
        out_specs.append(d_out)
        out_shape.append(d_shape)
        scratch += d_scratch
    return pl.pallas_call(
        functools.partial(_hgrn_kernel, c=c, heads=heads, dk=dk, dv=dv, dec=dec),
        grid_spec=pltpu.PrefetchScalarGridSpec(
            num_scalar_prefetch=1, grid=(batch, nc), in_specs=in_specs, out_specs=out_specs,
            scratch_shapes=scratch),
        out_shape=out_shape,
        compiler_params=_cparams("parallel" if dec is None else "arbitrary", "arbitrary"),
        name="hgrn_prompt",
    )(page_table, p["hq"], p["a_k"], p["a_logf"], p["hi"], p["sg_h"], hgrn_norm_w.reshape(1, dv),
      *dec_ops)


def _hgrn_step_kernel(q_ref, k_ref, g_ref, v_ref, sg_ref, nw_ref, s0_ref, o_ref, s_ref, *, heads):
    pad = lambda x: jnp.concatenate([x, jnp.zeros((LANES - heads, x.shape[1]), F32)], axis=0)
    qt = pad(q_ref[...].astype(F32)).T
    kt = pad(k_ref[...].astype(F32)).T
    ft = pad(jnp.exp(g_ref[...])).T
    v = v_ref[...].astype(F32)
    sg = sg_ref[...].astype(F32)
    rows = []
    for h in range(heads):
        s = ft[:, h:h + 1] * s0_ref[h] + kt[:, h:h + 1] * v[h:h + 1, :]
        s_ref[h] = s
        o = jnp.sum(qt[:, h:h + 1] * s, axis=0, keepdims=True)
        rows.append(_rms(o, nw_ref[...]) * sg[h:h + 1, :])
    o_ref[...] = jnp.concatenate(rows, axis=0).astype(BF16)


def _hgrn_step_call(p, hgrn_norm_w, state, heads, dk, dv):
    db = state.shape[0]
    r3 = lambda x, w: x.reshape(db, heads, w)
    blk = lambda w: pl.BlockSpec((None, heads, w), lambda b: (b, 0, 0))
    st_blk = pl.BlockSpec((None, heads, dk, dv), lambda b: (b, 0, 0, 0))
    o, s = pl.pallas_call(
        functools.partial(_hgrn_step_kernel, heads=heads),
        grid=(db,),
        in_specs=[blk(dk), blk(dk), blk(dk), blk(dv), blk(dv),
                  pl.BlockSpec((1, dv), lambda b: (0, 0)), st_blk],
        out_specs=(blk(dv), st_blk),
        out_shape=(jax.ShapeDtypeStruct((db, heads, dv), BF16),
                   jax.ShapeDtypeStruct(state.shape, F32)),
        compiler_params=_cparams("parallel"),
        name="hgrn_step",
    )(r3(p["hq"], dk), r3(p["a_k"], dk), r3(p["a_logf"], dk), r3(p["hi"], dv), r3(p["sg_h"], dv),
      hgrn_norm_w.reshape(1, dv), state)
    return o.reshape(db, heads * dv), s


def _fox_kernel(pt_ref, q_ref, k_ref, vt_ref, sg_ref, *rest, t, dec):
    qi = pl.program_id(2)
    if dec is None:
        o_ref, s_ref, m_ref, l_ref, acc_ref = rest
        dec_state = None
    else:
        n_in = _dec_num_inputs(dec)
        o_ref, do_ref, s_ref, m_ref, l_ref, acc_ref = rest[n_in:n_in + 6]
        lin = ((pl.program_id(0) * pl.num_programs(1) + pl.program_id(1)) * pl.num_programs(2) + qi)
        dec_args = (dec, pt_ref, lin, rest[:n_in], do_ref, rest[n_in + 6:])
        dec_state = _dec_main(*dec_args)
    q = q_ref[...]
    m_ref[...] = jnp.full_like(m_ref, -jnp.inf)
    l_ref[...] = jnp.zeros_like(l_ref)
    acc_ref[...] = jnp.zeros_like(acc_ref)

    def scores(j):
        return _dot_nt(k_ref[pl.ds(pl.multiple_of(j * t, t), t), :], q)

    def update(j, s):
        m_prev = m_ref[...]
        m_new = jnp.maximum(m_prev, jnp.max(s, axis=0, keepdims=True))
        alpha = jnp.exp2(m_prev - m_new)
        p = jnp.exp2(s - m_new)
        l_ref[...] = alpha * l_ref[...] + jnp.sum(p, axis=0, keepdims=True)
        acc_ref[...] = alpha * acc_ref[...] + _dot(vt_ref[j], p.astype(BF16))
        m_ref[...] = m_new

    s_ref[...] = scores(0)

    def body(j, carry):
        s = s_ref[...]
        s_next = scores(j + 1)
        update(j, s)
        s_ref[...] = s_next
        return carry

    lax.fori_loop(0, qi, body, 0)
    key = lax.broadcasted_iota(jnp.int32, (t, t), 0)
    qry = lax.broadcasted_iota(jnp.int32, (t, t), 1)
    update(qi, jnp.where(key <= qry, s_ref[...], -jnp.inf))
    o = (acc_ref[...] / l_ref[...]).T
    o_ref[...] = (o * sg_ref[...].astype(F32)).astype(BF16)
    if dec is not None:
        _dec_finish(*dec_args, dec_state)


def _fox_steps(batch, seq, heads, t):
    return batch * heads * (seq // t)


def _fox_prompt_call(p, batch, seq, heads, hd, t, page_table, dec=None, dec_ops=()):
    nq = seq // t
    in_specs = [pl.BlockSpec((t, 2 * hd), lambda b, h, i, pt: (b * nq + i, h)),
                pl.BlockSpec((seq, 2 * hd), lambda b, h, i, pt: (b, h)),
                pl.BlockSpec((nq, hd, t), lambda b, h, i, pt: (b, h, 0)),
                pl.BlockSpec((t, hd), lambda b, h, i, pt: (b * nq + i, h))]
    out_specs = [pl.BlockSpec((t, hd), lambda b, h, i, pt: (b * nq + i, h))]
    out_shape = [jax.ShapeDtypeStruct((batch * seq, heads * hd), BF16)]
    scratch = [pltpu.VMEM((t, t), F32), pltpu.VMEM((1, t), F32), pltpu.VMEM((1, t), F32),
               pltpu.VMEM((hd, t), F32)]
    if dec is not None:
        d_in, d_out, d_shape, d_scratch = _dec_specs(dec, lambda b, h, i: (b * heads + h) * nq + i)
        in_specs += d_in
        out_specs.append(d_out)
        out_shape.append(d_shape)
        scratch += d_scratch
    return pl.pallas_call(
        functools.partial(_fox_kernel, t=t, dec=dec),
        grid_spec=pltpu.PrefetchScalarGridSpec(
            num_scalar_prefetch=1, grid=(batch, heads, nq), in_specs=in_specs, out_specs=out_specs,
            scratch_shapes=scratch),
        out_shape=out_shape,
        compiler_params=_cparams(*(["parallel" if dec is None else "arbitrary"] * 2), "arbitrary"),
        name="fox_prompt",
    )(page_table, p["q_aug"], p["k_aug"], p["v_t"], p["sg_f"], *dec_ops)


def _page_bias_kernel(x_ref, u_ref, rho_ref, tot_ref, *, heads):
    x = x_ref[...]
    rho_ref[...] = _dot01_nt(x, u_ref[...])
    n = x.shape[1]
    t = x[:, :LANES]
    for i in range(1, n // LANES):
        t = t + x[:, i * LANES:(i + 1) * LANES]
    shift = heads
    while shift < LANES:
        t = t + pltpu.roll(t, shift, 1)
        shift *= 2
    tot_ref[...] = jnp.concatenate([t] * (n // LANES), axis=1)


def _page_bias_call(lf_flat, heads):
    n_pages, n = lf_flat.shape
    rows = n // heads
    assert LANES % heads == 0 and n % LANES == 0
    idx = jnp.arange(n, dtype=jnp.int32)
    later = (idx[:, None] // heads) > (idx[None, :] // heads)
    same_head = (idx[:, None] % heads) == (idx[None, :] % heads)
    u = (later & same_head).astype(BF16)
    pb = _largest_tile(n_pages, BIAS_PAGE_TILE, SUBLANES)
    blk = pl.BlockSpec((pb, n), lambda i: (i, 0))
    del rows
    return pl.pallas_call(
        functools.partial(_page_bias_kernel, heads=heads),
        grid=(n_pages // pb,),
        in_specs=[blk, pl.BlockSpec((n, n), lambda i: (0, 0))],
        out_specs=(blk, blk),
        out_shape=(jax.ShapeDtypeStruct((n_pages, n), F32),) * 2,
        compiler_params=_cparams("parallel"),
        name="page_bias",
    )(lf_flat, u)


class _Dec(NamedTuple):
    layer: int
    seq0: int
    nseq: int
    n_pg: int
    pps: int
    rows: int
    heads: int
    hd: int
    steps: int

    @property
    def groups(self):
        return self.n_pg // self.pps

    @property
    def guarded(self):
        return self.steps != self.nseq * self.groups


def _dec_where(dec, lin):
    active = lin < dec.nseq * dec.groups
    local = jnp.minimum(lin // dec.groups, dec.nseq - 1)
    grp = jnp.where(active, lin % dec.groups, dec.groups - 1)
    return active, local, grp


def _dec_num_inputs(dec):
    return 5 + 4 * dec.pps


def _dec_specs(dec, lin_of):
    n = dec.rows * dec.heads

    def seq_blk(shape, base):
        def imap(*a):
            return (base + _dec_where(dec, lin_of(*a[:-1]))[1], 0, 0)
        return pl.BlockSpec((None,) + shape, imap)

    def page(a, slot):
        _, local, grp = _dec_where(dec, lin_of(*a[:-1]))
        return a[-1][dec.seq0 + local, dec.n_pg - 1 - (grp * dec.pps + slot)]

    def cache_blk(slot):
        return pl.BlockSpec((None, None, dec.rows, dec.heads, dec.hd),
                            lambda *a: (dec.layer, page(a, slot), 0, 0, 0))

    def bias_blk(slot):
        return pl.BlockSpec((SUBLANES, n), lambda *a: (page(a, slot) // SUBLANES, 0))

    slots = range(dec.pps)
    in_specs = ([seq_blk((HEAD_PAD, dec.hd), dec.seq0)] * 3 + [seq_blk((1, n), dec.seq0)]
                + [seq_blk((HEAD_PAD, dec.hd), dec.seq0)]
                + [cache_blk(s) for s in slots] + [cache_blk(s) for s in slots]
                + [bias_blk(s) for s in slots] + [bias_blk(s) for s in slots])
    scratch = [pltpu.VMEM((HEAD_PAD, LANES), F32), pltpu.VMEM((HEAD_PAD, LANES), F32),
               pltpu.VMEM((1, n), F32), pltpu.VMEM((HEAD_PAD, dec.hd), F32)]
    out_shape = jax.ShapeDtypeStruct((dec.nseq, HEAD_PAD, dec.hd), BF16)
    return in_specs, seq_blk((HEAD_PAD, dec.hd), 0), out_shape, scratch


def _dec_pages(dec, pt_ref, local, grp, in_refs, scratch):
    pps, heads, hd = dec.pps, dec.heads, dec.hd
    q_ref, cn_ref = in_refs[0], in_refs[3]
    k_refs = in_refs[5:5 + pps]
    v_refs = in_refs[5 + pps:5 + 2 * pps]
    rho_refs = in_refs[5 + 2 * pps:5 + 3 * pps]
    tot_refs = in_refs[5 + 3 * pps:5 + 4 * pps]
    m_ref, l_ref, run_ref, acc_ref = scratch
    n = dec.rows * heads

    @pl.when(grp == 0)
    def _():
        m_ref[...] = jnp.full_like(m_ref, -jnp.inf)
        l_ref[...] = jnp.zeros_like(l_ref)
        run_ref[...] = jnp.zeros_like(run_ref)
        acc_ref[...] = jnp.zeros_like(acc_ref)

    q = q_ref[...]
    sub = lax.broadcasted_iota(jnp.int32, (HEAD_PAD, n), 0)
    lane = lax.broadcasted_iota(jnp.int32, (HEAD_PAD, n), 1)
    own = (lane % heads) == (sub % heads)
    cn = cn_ref[...]
    run = run_ref[...]
    scores = []
    for p in range(pps):
        r = pt_ref[dec.seq0 + local, dec.n_pg - 1 - (grp * pps + p)] % SUBLANES
        kb = k_refs[p][...].reshape(n, hd).astype(BF16)
        bias = (rho_refs[p][pl.ds(r, 1), :] + run + cn) * LOG2E
        scores.append(jnp.where(own, _dot_nt(q, kb) + bias, -jnp.inf))
        run = run + tot_refs[p][pl.ds(r, 1), :]
    run_ref[...] = run

    m_prev = m_ref[:, :1]
    m_new = m_prev
    for s in scores:
        m_new = jnp.maximum(m_new, jnp.max(s, axis=-1, keepdims=True))
    alpha = jnp.exp2(m_prev - m_new)
    l_new = alpha * l_ref[:, :1]
    acc = alpha * acc_ref[...]
    for p in range(pps):
        pr = jnp.exp2(scores[p] - m_new)
        l_new = l_new + jnp.sum(pr, axis=-1, keepdims=True)
        acc = acc + _dot(pr.astype(BF16), v_refs[p][...].reshape(n, hd).astype(BF16))
    m_ref[...] = jnp.broadcast_to(m_new, m_ref.shape)
    l_ref[...] = jnp.broadcast_to(l_new, l_ref.shape)
    acc_ref[...] = acc
    return m_new, l_new, acc


def _dec_new_row(dec, grp, in_refs, o_ref, state):
    q_ref, knew_ref, vnew_ref, _, sg_ref = in_refs[:5]
    m_new, l_new, acc = state

    @pl.when(grp == dec.groups - 1)
    def _():
        s_new = jnp.sum(q_ref[...].astype(F32) * knew_ref[...].astype(F32), axis=-1, keepdims=True)
        m_fin = jnp.maximum(m_new, s_new)
        a_fin = jnp.exp2(m_new - m_fin)
        p_new = jnp.exp2(s_new - m_fin)
        l_fin = a_fin * l_new + p_new
        out = (a_fin * acc + p_new * vnew_ref[...]) / l_fin
        o_ref[...] = (out * sg_ref[...].astype(F32)).astype(BF16)


def _dec_main(dec, pt_ref, lin, in_refs, o_ref, scratch):
    active, local, grp = _dec_where(dec, lin)
    if not dec.guarded:
        return _dec_pages(dec, pt_ref, local, grp, in_refs, scratch)

    @pl.when(active)
    def _():
        state = _dec_pages(dec, pt_ref, local, grp, in_refs, scratch)
        _dec_new_row(dec, grp, in_refs, o_ref, state)
    return None


def _dec_finish(dec, pt_ref, lin, in_refs, o_ref, scratch, state):
    del pt_ref, scratch
    if state is not None:
        _dec_new_row(dec, _dec_where(dec, lin)[2], in_refs, o_ref, state)


def _decode_kernel(pt_ref, *refs, dec):
    n_in = _dec_num_inputs(dec)
    args = (dec, pt_ref, pl.program_id(0), refs[:n_in], refs[n_in], refs[n_in + 1:])
    _dec_finish(*args, _dec_main(*args))


def _decode_operands(p, cache_k, cache_v, rho, tot, pps, heads, hd):
    db = p["f_logf"].shape[0]
    rows = cache_k.shape[2]
    padh = lambda x: jnp.pad(x, ((0, 0), (0, HEAD_PAD - heads), (0, 0)))
    head_rows = lambda x: padh(x.reshape(db, heads, hd))
    q16 = padh(p["q_aug"].reshape(db, heads, 2 * hd)[:, :, :hd])
    k16 = padh(p["k_aug"].reshape(db, heads, 2 * hd)[:, :, :hd])
    cn = jnp.tile(p["f_logf"], (1, rows)).reshape(db, 1, rows * heads)
    return ([q16, k16, head_rows(p["f_v"]), cn, head_rows(p["sg_f"])]
            + [cache_k] * pps + [cache_v] * pps + [rho] * pps + [tot] * pps)


def _decode_call(dec, dec_ops, page_table):
    in_specs, out_spec, out_shape, scratch = _dec_specs(dec, lambda i: i)
    return pl.pallas_call(
        functools.partial(_decode_kernel, dec=dec),
        grid_spec=pltpu.PrefetchScalarGridSpec(
            num_scalar_prefetch=1, grid=(dec.steps,), in_specs=in_specs, out_specs=out_spec,
            scratch_shapes=scratch),
        out_shape=out_shape,
        compiler_params=_cparams("arbitrary"),
        name="fox_decode",
    )(page_table, *dec_ops)


def _out_kernel(a_ref, b_ref, wa_ref, wb_ref, x_ref, gate_ref, y_ref):
    out = _dot(a_ref[...], wa_ref[...]) + _dot(b_ref[...], wb_ref[...])
    y_ref[...] = x_ref[...] + gate_ref[...] * out


def _out_call(a, b, w_out_bf, x2, mod3, tm, tiles_per_group):
    m, d = x2.shape
    da, db_ = a.shape[1], b.shape[1]
    assert da == db_, "w_out is read as two equal row blocks"
    return pl.pallas_call(
        _out_kernel,
        grid=(m // tm,),
        in_specs=[pl.BlockSpec((tm, da), lambda i: (i, 0)),
                  pl.BlockSpec((tm, db_), lambda i: (i, 0)),
                  pl.BlockSpec((da, d), lambda i: (0, 0)),
                  pl.BlockSpec((db_, d), lambda i: (1, 0)),
                  pl.BlockSpec((tm, d), lambda i: (i, 0)),
                  _mod_spec(mod3, d, 2, tiles_per_group)],
        out_specs=pl.BlockSpec((tm, d), lambda i: (i, 0)),
        out_shape=jax.ShapeDtypeStruct((m, d), F32),
        compiler_params=_cparams("parallel"),
        name="out_proj",
    )(a, b, w_out_bf, w_out_bf, x2, mod3)


def kernel(x_prompt, x_sample, c_prompt, c_sample, cache_k, cache_v, cache_logf, state_hgrn,
           page_table, norm_w, w_ada, b_ada, w_in, b_fox_f, lb_logits, q_norm_w, k_norm_w,
           hgrn_norm_w, w_out):
    batch, seq, d = x_prompt.shape
    db, dseq, _ = x_sample.shape
    assert dseq == 1, "the sample group is one new row per sequence"
    depth, n_phys, page_rows, fheads, hd = cache_k.shape
    _, _, hheads, dk, dv = state_hgrn.shape
    hk, dh, dfox = hheads * dk, hheads * dv, fheads * hd
    n_seg = 2 * hk + 2 * dh + 4 * dfox
    assert w_in.shape[2] == n_seg + fheads
    mp, ms = batch * seq, db * dseq
    tm = _largest_tile(seq, ROW_TILE, LANES)

    xp = x_prompt.reshape(mp, d)
    xs = x_sample.reshape(ms, d)
    r_pad = -(batch + db) % SUBLANES
    c_all = jnp.concatenate([c_prompt, c_sample, jnp.zeros((r_pad, d), F32)], axis=0)

    outs = [[] for _ in range(8)]
    for l in range(depth):
        mod = _ada_call(c_all, w_ada[l], b_ada[l])
        mod_p = mod[:batch].reshape(batch, 1, 3 * d)
        mod_s = mod[batch:batch + db].reshape(1, db, 3 * d)
        w_bf = w_in[l].astype(BF16)
        w_ff = jnp.pad(w_bf[:, n_seg:], ((0, 0), (0, LANES - fheads)))
        b_ff = jnp.pad(b_fox_f[l].astype(F32), (0, LANES - fheads)).reshape(1, LANES)
        w_vt = w_bf[:, 2 * hk + 2 * dh + 2 * dfox:2 * hk + 2 * dh + 3 * dfox].T
        w_out_bf = w_out[l].astype(BF16)
        proj = functools.partial(_in_proj, w_bf=w_bf, w_ff=w_ff, b_ff=b_ff, lb_logits=lb_logits,
                                 q_norm_w=q_norm_w[l], k_norm_w=k_norm_w[l], layer=l,
                                 hgrn_heads=hheads, dk=dk, fox_heads=fheads, hd=hd)

        hp = _norm_call(xp, norm_w[l], mod_p, tm, seq // tm)
        pp = proj(hp, tm=tm, seq_len=seq, w_vt=w_vt)
        hs = _norm_call(xs, norm_w[l], mod_s, ms, 1)
        ps = proj(hs, tm=ms, seq_len=1)

        rho, tot = _page_bias_call(cache_logf[l].reshape(n_phys, page_rows * fheads), fheads)
        n_pg = page_table.shape[1]
        pps = _largest_tile(n_pg, PAGES_PER_STEP, 1)
        dec_ops = _decode_operands(ps, cache_k, cache_v, rho, tot, pps, fheads, hd)
        fox_steps, hgrn_steps = _fox_steps(batch, seq, fheads, tm), _hgrn_steps(batch, seq)
        n_fox = min(db - db // 2, fox_steps // (n_pg // pps))
        n_hgrn = min(db - n_fox, hgrn_steps // (n_pg // pps))
        n_rest = db - n_fox - n_hgrn
        plan = lambda seq0, nseq, steps: None if nseq == 0 else _Dec(
            layer=l, seq0=seq0, nseq=nseq, n_pg=n_pg, pps=pps, rows=page_rows, heads=fheads, hd=hd,
            steps=steps)
        dec_fox = plan(0, n_fox, fox_steps)
        dec_hgrn = plan(n_fox, n_hgrn, hgrn_steps)
        dec_rest = plan(n_fox + n_hgrn, n_rest, n_rest * (n_pg // pps))
        hosted = lambda dec: () if dec is None else tuple(dec_ops)

        a_p, s_p, *o_hgrn = _hgrn_prompt_call(pp, hgrn_norm_w[l], batch, seq, hheads, dk, dv,
                                              page_table, dec_hgrn, hosted(dec_hgrn))
        b_p, *o_fox = _fox_prompt_call(pp, batch, seq, fheads, hd, tm, page_table, dec_fox,
                                       hosted(dec_fox))
        xp = _out_call(a_p, b_p, w_out_bf, xp, mod_p, tm, seq // tm)
        outs[0].append(pp["f_k"].reshape(batch, seq, fheads, hd))
        outs[1].append(pp["f_v"].reshape(batch, seq, fheads, hd))
        outs[2].append(pp["f_logf"].reshape(batch, seq, fheads))
        outs[3].append(s_p)

        a_s, s_s = _hgrn_step_call(ps, hgrn_norm_w[l], state_hgrn[l], hheads, dk, dv)
        o_rest = [] if dec_rest is None else [_decode_call(dec_rest, dec_ops, page_table)]
        b_s = jnp.concatenate(o_fox + o_hgrn + o_rest, axis=0)[:, :fheads, :].reshape(db, dfox)
        xs = _out_call(a_s, b_s, w_out_bf, xs, mod_s, ms, 1)
        outs[4].append(ps["f_k"].reshape(db, dseq, fheads, hd))
        outs[5].append(ps["f_v"].reshape(db, dseq, fheads, hd))
        outs[6].append(ps["f_logf"].reshape(db, dseq, fheads))
        outs[7].append(s_s)

    stk = [jnp.stack(o) for o in outs]
    return (xp.reshape(batch, seq, d), xs.reshape(db, dseq, d), *stk)
```

```python
import functools
from typing import NamedTuple

import jax
import jax.numpy as jnp
from jax import lax
from jax.experimental import pallas as pl
from jax.experimental.pallas import tpu as pltpu

F32 = jnp.float32
BF16 = jnp.bfloat16

LANES = 128
SUBLANES = 8
VMEM_LIMIT = 56 * 1024 * 1024
RMS_EPS = 1e-6
LOG2E = 1.4426950408889634
HGRN_CHUNK = 64
ROW_TILE = 512
PAGES_PER_STEP = 8
BIAS_PAGE_TILE = 256
HEAD_PAD = 16


def _cparams(*sem):
    return pltpu.CompilerParams(dimension_semantics=sem, vmem_limit_bytes=VMEM_LIMIT)


def _dot(a, b):
    return jnp.dot(a, b, preferred_element_type=F32)


def _dot_nt(a, b):
    return lax.dot_general(a, b, (((1,), (1,)), ((), ())), preferred_element_type=F32)


def _dot_tn(a, b):
    return lax.dot_general(a, b, (((0,), (0,)), ((), ())), preferred_element_type=F32)


def _split3(x):
    hi = x.astype(BF16)
    r = x - hi.astype(F32)
    mid = r.astype(BF16)
    lo = (r - mid.astype(F32)).astype(BF16)
    return hi, mid, lo


def _dot01(m01, x):
    hi, mid, lo = _split3(x)
    return _dot(m01, hi) + _dot(m01, mid) + _dot(m01, lo)


def _dot01_nt(x, m01):
    hi, mid, lo = _split3(x)
    return _dot(hi, m01) + _dot(mid, m01) + _dot(lo, m01)


def _sigmoid(x):
    return 1.0 / (1.0 + jnp.exp(-x))


def _silu(x):
    return x * _sigmoid(x)


def _rms(x, w):
    ms = jnp.mean(x * x, axis=-1, keepdims=True)
    return x * lax.rsqrt(ms + RMS_EPS) * w


def _largest_tile(n, cap, align):
    for t in range(min(cap, n), 0, -1):
        if n % t == 0 and t % align == 0:
            return t
    return n


def _ada_kernel(c_ref, w_ref, b_ref, o_ref):
    s = _silu(c_ref[...])
    s_hi = s.astype(BF16)
    s_lo = (s - s_hi.astype(F32)).astype(BF16)
    w = w_ref[...]
    w_hi = w.astype(BF16)
    w_lo = (w - w_hi.astype(F32)).astype(BF16)
    o_ref[...] = _dot(s_hi, w_hi) + _dot(s_hi, w_lo) + _dot(s_lo, w_hi) + b_ref[...]


def _ada_call(c_all, w_ada, b_ada):
    r, d = c_all.shape
    n = w_ada.shape[1]
    tn = 512
    return pl.pallas_call(
        _ada_kernel,
        grid=(n // tn,),
        in_specs=[pl.BlockSpec((r, d), lambda j: (0, 0)),
                  pl.BlockSpec((d, tn), lambda j: (0, j)),
                  pl.BlockSpec((1, tn), lambda j: (0, j))],
        out_specs=pl.BlockSpec((r, tn), lambda j: (0, j)),
        out_shape=jax.ShapeDtypeStruct((r, n), F32),
        compiler_params=_cparams("parallel"),
        name="ada_mod",
    )(c_all, w_ada, b_ada.reshape(1, n))


def _norm_kernel(x_ref, nw_ref, shift_ref, scale_ref, h_ref):
    xn = _rms(x_ref[...], nw_ref[...])
    h_ref[...] = (xn * (1.0 + scale_ref[...]) + shift_ref[...]).astype(BF16)


def _mod_spec(mod3, d, col, tiles_per_group):
    return pl.BlockSpec((None, mod3.shape[1], d), lambda i: (i // tiles_per_group, 0, col))


def _norm_call(x2, norm_w, mod3, tm, tiles_per_group):
    m, d = x2.shape
    return pl.pallas_call(
        _norm_kernel,
        grid=(m // tm,),
        in_specs=[pl.BlockSpec((tm, d), lambda i: (i, 0)),
                  pl.BlockSpec((1, d), lambda i: (0, 0)),
                  _mod_spec(mod3, d, 0, tiles_per_group),
                  _mod_spec(mod3, d, 1, tiles_per_group)],
        out_specs=pl.BlockSpec((tm, d), lambda i: (i, 0)),
        out_shape=jax.ShapeDtypeStruct((m, d), BF16),
        compiler_params=_cparams("parallel"),
        name="norm_mod",
    )(x2, norm_w.reshape(1, d), mod3, mod3)


def _proj_plain_kernel(h_ref, w_ref, o_ref, *, act):
    z = _dot(h_ref[...], w_ref[...])
    if act:
        z = _silu(z)
    o_ref[...] = z.astype(o_ref.dtype)


def _proj_v_kernel(h_ref, w_ref, o_ref):
    o_ref[...] = _dot(h_ref[...], w_ref[...])


def _proj_vt_kernel(h_ref, w_ref, o_ref, ot_ref):
    z = _dot(h_ref[...], w_ref[...])
    o_ref[...] = z
    ot_ref[...] = z.T.astype(BF16)


def _proj_forget_kernel(h_ref, w_ref, lbl_ref, logf_ref, k_ref, *, layer):
    z = _dot(h_ref[...], w_ref[...])
    lbl = lbl_ref[...]
    e = jnp.exp(lbl - jnp.max(lbl, axis=0, keepdims=True))
    lb = jnp.sum(e[:layer + 1], axis=0, keepdims=True) / jnp.sum(e, axis=0, keepdims=True)
    t = jnp.exp(-jnp.abs(z))
    big = 1.0 / (1.0 + t)
    small = t * big
    pos = z >= 0.0
    logf_ref[...] = jnp.log(lb + (1.0 - lb) * jnp.where(pos, big, small))
    k_ref[...] = ((1.0 - lb) * jnp.where(pos, small, big)).astype(BF16)


def _aug_lanes(c2, is_query, tm):
    hi, mid, lo = _split3(c2)
    hi, mid, lo = hi.astype(F32), mid.astype(F32), lo.astype(F32)
    lane = lax.broadcasted_iota(jnp.int32, (tm, LANES), 1)
    one = jnp.ones((tm, LANES), F32)
    zero = jnp.zeros((tm, LANES), F32)
    if is_query:
        parts = jnp.where(lane == 0, hi, jnp.where(lane == 1, mid, jnp.where(lane == 2, lo, zero)))
        return jnp.where((lane >= 3) & (lane < 6), one, parts).astype(BF16)
    parts = jnp.where(lane == 3, -hi, jnp.where(lane == 4, -mid, jnp.where(lane == 5, -lo, zero)))
    return jnp.where(lane < 3, one, parts).astype(BF16)


def _proj_qk_kernel(h_ref, w_ref, nw_ref, c_ref, *out_refs, heads, hd, is_query, qscale):
    z = _dot(h_ref[...], w_ref[...])
    tm = z.shape[0]
    nw = nw_ref[...]
    c = c_ref[...]
    for h in range(heads):
        n = _rms(z[:, h * hd:(h + 1) * hd], nw)
        aug = _aug_lanes(c[:, h:h + 1] * LOG2E, is_query, tm)
        if is_query:
            (aug_ref,) = out_refs
            n = n * qscale
        else:
            full_ref, aug_ref = out_refs
            full_ref[:, h * hd:(h + 1) * hd] = n
        aug_ref[:, 2 * h * hd:(2 * h + 1) * hd] = n.astype(BF16)
        aug_ref[:, (2 * h + 1) * hd:(2 * h + 2) * hd] = aug


def _proj_fgate_kernel(h_ref, w_ref, b_ref, logf_ref, c_ref, carry_ref, *, heads, tiles_per_seq):
    x = _dot(h_ref[...], w_ref[...]) + b_ref[...]
    lf = jnp.minimum(x, 0.0) - jnp.log1p(jnp.exp(-jnp.abs(x)))
    logf_ref[...] = lf[:, :heads]
    if tiles_per_seq is None:
        c_ref[...] = lf[:, :heads]
        return
    tm = lf.shape[0]

    @pl.when(pl.program_id(0) % tiles_per_seq == 0)
    def _():
        carry_ref[...] = jnp.zeros_like(carry_ref)

    row = lax.broadcasted_iota(jnp.int32, (tm, tm), 0)
    col = lax.broadcasted_iota(jnp.int32, (tm, tm), 1)
    tril = jnp.where(col <= row, 1.0, 0.0).astype(BF16)
    c = _dot01(tril, lf) + carry_ref[...]
    c_ref[...] = c[:, :heads]
    carry_ref[...] = c[tm - 1:tm, :]


def _proj_call(kernel, h, w, seg_w, seg, extra_inputs, extra_specs, out_shapes, out_specs, tm,
               scratch=(), sem="parallel", name="in_proj"):
    m, d = h.shape
    return pl.pallas_call(
        kernel,
        grid=(m // tm,),
        in_specs=[pl.BlockSpec((tm, d), lambda i: (i, 0)),
                  pl.BlockSpec((d, seg_w), lambda i: (0, seg))] + list(extra_specs),
        out_specs=out_specs,
        out_shape=out_shapes,
        scratch_shapes=list(scratch),
        compiler_params=_cparams(sem),
        name=name,
    )(h, w, *extra_inputs)


def _full_spec(shape):
    nd = len(shape)
    return pl.BlockSpec(shape, lambda i: (0,) * nd)


def _in_proj(h, w_bf, w_ff, b_ff, lb_logits, q_norm_w, k_norm_w, layer, tm, seq_len,
             hgrn_heads, dk, fox_heads, hd, with_vt=False):
    m, d = h.shape
    hk = hgrn_heads * dk
    dfox = fox_heads * hd
    row = lambda width: pl.BlockSpec((tm, width), lambda i: (i, 0))
    sds = lambda width, dt: jax.ShapeDtypeStruct((m, width), dt)
    assert hk == dfox, "segments are indexed as equal-width column blocks"
    plain = lambda seg, act, nm: _proj_call(
        functools.partial(_proj_plain_kernel, act=act), h, w_bf, hk, seg, (), (),
        sds(hk, BF16), row(hk), tm, name=nm)
    hq = plain(0, False, "proj_hq")
    a_logf, a_k = _proj_call(
        functools.partial(_proj_forget_kernel, layer=layer), h, w_bf, hk, 1,
        (lb_logits,), (_full_spec(lb_logits.shape),),
        (sds(hk, F32), sds(hk, BF16)), (row(hk), row(hk)), tm, name="proj_hf")
    hi = plain(2, False, "proj_hi")
    sg_h = plain(3, True, "proj_hg")
    tiles_per_seq = None if seq_len == 1 else seq_len // tm
    f_logf, f_c = _proj_call(
        functools.partial(_proj_fgate_kernel, heads=fox_heads, tiles_per_seq=tiles_per_seq),
        h, w_ff, LANES, 0, (b_ff,), (_full_spec(b_ff.shape),),
        (sds(fox_heads, F32), sds(fox_heads, F32)), (row(fox_heads), row(fox_heads)), tm,
        scratch=(pltpu.VMEM((1, LANES), F32),), sem="arbitrary", name="proj_ff")
    qk = lambda seg, nw, is_query, outs, specs, nm: _proj_call(
        functools.partial(_proj_qk_kernel, heads=fox_heads, hd=hd, is_query=is_query,
                          qscale=hd ** -0.5 * LOG2E),
        h, w_bf, dfox, seg, (nw.reshape(1, hd), f_c), (_full_spec((1, hd)), row(fox_heads)),
        outs, specs, tm, name=nm)
    q_aug = qk(4, q_norm_w, True, sds(2 * dfox, BF16), row(2 * dfox), "proj_fq")
    f_k, k_aug = qk(5, k_norm_w, False, (sds(dfox, F32), sds(2 * dfox, BF16)),
                    (row(dfox), row(2 * dfox)), "proj_fk")
    if not with_vt:
        f_v = _proj_call(_proj_v_kernel, h, w_bf, dfox, 6, (), (), sds(dfox, F32), row(dfox), tm,
                         name="proj_fv")
        v_t = None
    else:
        f_v, v_t = _proj_call(
            _proj_vt_kernel, h, w_bf, dfox, 6, (), (),
            (sds(dfox, F32), jax.ShapeDtypeStruct((m // tm, dfox, tm), BF16)),
            (row(dfox), pl.BlockSpec((None, dfox, tm), lambda i: (i, 0, 0))), tm, name="proj_fv")
    sg_f = plain(7, True, "proj_fg")
    return dict(hq=hq, a_logf=a_logf, a_k=a_k, hi=hi, sg_h=sg_h, f_logf=f_logf, f_c=f_c,
                q_aug=q_aug, f_k=f_k, k_aug=k_aug, f_v=f_v, v_t=v_t, sg_f=sg_f)


def _hgrn_levels(c):
    levels = []
    m = c // 2
    while m >= 1:
        levels.append(m)
        m //= 2
    return levels


def _hgrn_level_exponents(b, g, c):
    dk = b.shape[1]
    ridx = lax.broadcasted_iota(jnp.int32, (c, dk), 0)
    out = []
    for m in _hgrn_levels(c):
        pos = ridx & (2 * m - 1)
        later = pos >= m
        if 2 * m >= SUBLANES:
            b3 = b.reshape(c // (2 * m), 2 * m, dk)
            bm = jnp.broadcast_to(b3[:, m - 1:m, :], b3.shape).reshape(c, dk)
            e = jnp.where(later, b - bm, bm - b)
        elif m == 2:
            g_next = pltpu.roll(g, c - 1, 0)
            g_prev = pltpu.roll(g, 1, 0)
            e = jnp.where(pos == 0, g_next,
                          jnp.where(pos == 2, g, jnp.where(pos == 3, g + g_prev, 0.0)))
        else:
            e = jnp.where(later, g, 0.0)
        out.append((e, later))
    return out


def _hgrn_kernel(pt_ref, q_ref, k_ref, g_ref, v_ref, sg_ref, nw_ref, *rest, c, heads, dk, dv, dec):
    ci = pl.program_id(1)
    if dec is None:
        o_ref, s_ref, st_ref = rest
    else:
        n_in = _dec_num_inputs(dec)
        (o_ref, s_ref, do_ref), st_ref = rest[n_in:n_in + 3], rest[n_in + 3]
        lin = pl.program_id(0) * pl.num_programs(1) + ci
        dec_args = (dec, pt_ref, lin, rest[:n_in], do_ref, rest[n_in + 4:])

    @pl.when(ci == 0)
    def _():
        st_ref[...] = jnp.zeros_like(st_ref)

    dec_state = None if dec is None else _dec_main(*dec_args)

    row = lax.broadcasted_iota(jnp.int32, (c, c), 0)
    col = lax.broadcasted_iota(jnp.int32, (c, c), 1)
    tril = jnp.where(col <= row, 1.0, 0.0).astype(BF16)
    eye = row == col
    masks = []
    for m in _hgrn_levels(c):
        same = (row // (2 * m)) == (col // (2 * m))
        masks.append(same & ((row & (2 * m - 1)) >= m) & ((col & (2 * m - 1)) < m))
    nw = nw_ref[...]

    for h in range(heads):
        ks = slice(h * dk, (h + 1) * dk)
        vs = slice(h * dv, (h + 1) * dv)
        q = q_ref[:, ks].astype(F32)
        k = k_ref[:, ks].astype(F32)
        g = g_ref[:, ks] * LOG2E
        v = v_ref[:, vs]
        b = _dot01(tril, g)
        b_last = b[c - 1:c]
        a = jnp.where(eye, _dot_nt(q.astype(BF16), k.astype(BF16)), 0.0)
        for (e, later), mask in zip(_hgrn_level_exponents(b, g, c), masks):
            x = (jnp.where(later, q, k) * jnp.exp2(e)).astype(BF16)
            a = jnp.where(mask, _dot_nt(x, x), a)
        st = st_ref[h]
        o = _dot_nt((q * jnp.exp2(b)).astype(BF16), st.astype(BF16)) + _dot(a.astype(BF16), v)
        kd = (k * jnp.exp2(b_last - b)).astype(BF16)
        st_ref[h] = st * jnp.exp2(b_last) + _dot_tn(v, kd)
        o_ref[:, vs] = (_rms(o, nw) * sg_ref[:, vs].astype(F32)).astype(BF16)

    @pl.when(ci == pl.num_programs(1) - 1)
    def _():
        for h in range(heads):
            s_ref[h] = st_ref[h].T

    if dec is not None:
        _dec_finish(*dec_args, dec_state)


def _hgrn_steps(batch, seq):
    return batch * (seq // HGRN_CHUNK)


def _hgrn_prompt_call(p, hgrn_norm_w, batch, seq, heads, dk, dv, page_table, dec=None, dec_ops=()):
    c = HGRN_CHUNK
    nc = seq // c
    blk = lambda width: pl.BlockSpec((c, heads * width), lambda b, i, pt: (b * nc + i, 0))
    in_specs = [blk(dk), blk(dk), blk(dk), blk(dv), blk(dv),
                pl.BlockSpec((1, dv), lambda b, i, pt: (0, 0))]
    out_specs = [blk(dv), pl.BlockSpec((None, heads, dk, dv), lambda b, i, pt: (b, 0, 0, 0))]
    out_shape = [jax.ShapeDtypeStruct((batch * seq, heads * dv), BF16),
                 jax.ShapeDtypeStruct((batch, heads, dk, dv), F32)]
    scratch = [pltpu.VMEM((heads, dv, dk), F32)]
    if dec is not None:
        d_in, d_out, d_shape, d_scratch = _dec_specs(dec, lambda b, i: b * nc + i)
        in_specs += d_in
        out_specs.append(d_out)
        out_shape.append(d_shape)
        scratch += d_scratch
    return pl.pallas_call(
        functools.partial(_hgrn_kernel, c=c, heads=heads, dk=dk, dv=dv, dec=dec),
        grid_spec=pltpu.PrefetchScalarGridSpec(
            num_scalar_prefetch=1, grid=(batch, nc), in_specs=in_specs, out_specs=out_specs,
            scratch_shapes=scratch),
        out_shape=out_shape,
        compiler_params=_cparams("parallel" if dec is None else "arbitrary", "arbitrary"),
        name="hgrn_prompt",
    )(page_table, p["hq"], p["a_k"], p["a_logf"], p["hi"], p["sg_h"], hgrn_norm_w.reshape(1, dv),
      *dec_ops)


def _hgrn_step_kernel(q_ref, k_ref, g_ref, v_ref, sg_ref, nw_ref, s0_ref, o_ref, s_ref, *, heads):
    pad = lambda x: jnp.concatenate([x, jnp.zeros((LANES - heads, x.shape[1]), F32)], axis=0)
    qt = pad(q_ref[...].astype(F32)).T
    kt = pad(k_ref[...].astype(F32)).T
    ft = pad(jnp.exp(g_ref[...])).T
    v = v_ref[...].astype(F32)
    sg = sg_ref[...].astype(F32)
    rows = []
    for h in range(heads):
        s = ft[:, h:h + 1] * s0_ref[h] + kt[:, h:h + 1] * v[h:h + 1, :]
        s_ref[h] = s
        o = jnp.sum(qt[:, h:h + 1] * s, axis=0, keepdims=True)
        rows.append(_rms(o, nw_ref[...]) * sg[h:h + 1, :])
    o_ref[...] = jnp.concatenate(rows, axis=0).astype(BF16)


def _hgrn_step_call(p, hgrn_norm_w, state, heads, dk, dv):
    db = state.shape[0]
    r3 = lambda x, w: x.reshape(db, heads, w)
    blk = lambda w: pl.BlockSpec((None, heads, w), lambda b: (b, 0, 0))
    st_blk = pl.BlockSpec((None, heads, dk, dv), lambda b: (b, 0, 0, 0))
    o, s = pl.pallas_call(
        functools.partial(_hgrn_step_kernel, heads=heads),
        grid=(db,),
        in_specs=[blk(dk), blk(dk), blk(dk), blk(dv), blk(dv),
                  pl.BlockSpec((1, dv), lambda b: (0, 0)), st_blk],
        out_specs=(blk(dv), st_blk),
        out_shape=(jax.ShapeDtypeStruct((db, heads, dv), BF16),
                   jax.ShapeDtypeStruct(state.shape, F32)),
        compiler_params=_cparams("parallel"),
        name="hgrn_step",
    )(r3(p["hq"], dk), r3(p["a_k"], dk), r3(p["a_logf"], dk), r3(p["hi"], dv), r3(p["sg_h"], dv),
      hgrn_norm_w.reshape(1, dv), state)
    return o.reshape(db, heads * dv), s


def _fox_kernel(pt_ref, q_ref, k_ref, vt_ref, sg_ref, *rest, t, dec):
    qi = pl.program_id(2)
    if dec is None:
        o_ref, s_ref, m_ref, l_ref, acc_ref = rest
        dec_state = None
    else:
        n_in = _dec_num_inputs(dec)
        o_ref, do_ref, s_ref, m_ref, l_ref, acc_ref = rest[n_in:n_in + 6]
        lin = ((pl.program_id(0) * pl.num_programs(1) + pl.program_id(1)) * pl.num_programs(2) + qi)
        dec_args = (dec, pt_ref, lin, rest[:n_in], do_ref, rest[n_in + 6:])
        dec_state = _dec_main(*dec_args)
    q = q_ref[...]
    m_ref[...] = jnp.full_like(m_ref, -jnp.inf)
    l_ref[...] = jnp.zeros_like(l_ref)
    acc_ref[...] = jnp.zeros_like(acc_ref)

    def scores(j):
        return _dot_nt(k_ref[pl.ds(pl.multiple_of(j * t, t), t), :], q)

    def update(j, s):
        m_prev = m_ref[...]
        m_new = jnp.maximum(m_prev, jnp.max(s, axis=0, keepdims=True))
        alpha = jnp.exp2(m_prev - m_new)
        p = jnp.exp2(s - m_new)
        l_ref[...] = alpha * l_ref[...] + jnp.sum(p, axis=0, keepdims=True)
        acc_ref[...] = alpha * acc_ref[...] + _dot(vt_ref[j], p.astype(BF16))
        m_ref[...] = m_new

    s_ref[...] = scores(0)

    def body(j, carry):
        s = s_ref[...]
        s_next = scores(j + 1)
        update(j, s)
        s_ref[...] = s_next
        return carry

    lax.fori_loop(0, qi, body, 0)
    key = lax.broadcasted_iota(jnp.int32, (t, t), 0)
    qry = lax.broadcasted_iota(jnp.int32, (t, t), 1)
    update(qi, jnp.where(key <= qry, s_ref[...], -jnp.inf))
    o = (acc_ref[...] / l_ref[...]).T
    o_ref[...] = (o * sg_ref[...].astype(F32)).astype(BF16)
    if dec is not None:
        _dec_finish(*dec_args, dec_state)


def _fox_steps(batch, seq, heads, t):
    return batch * heads * (seq // t)


def _fox_prompt_call(p, batch, seq, heads, hd, t, page_table, dec=None, dec_ops=()):
    nq = seq // t
    in_specs = [pl.BlockSpec((t, 2 * hd), lambda b, h, i, pt: (b * nq + i, h)),
                pl.BlockSpec((seq, 2 * hd), lambda b, h, i, pt: (b, h)),
                pl.BlockSpec((nq, hd, t), lambda b, h, i, pt: (b, h, 0)),
                pl.BlockSpec((t, hd), lambda b, h, i, pt: (b * nq + i, h))]
    out_specs = [pl.BlockSpec((t, hd), lambda b, h, i, pt: (b * nq + i, h))]
    out_shape = [jax.ShapeDtypeStruct((batch * seq, heads * hd), BF16)]
    scratch = [pltpu.VMEM((t, t), F32), pltpu.VMEM((1, t), F32), pltpu.VMEM((1, t), F32),
               pltpu.VMEM((hd, t), F32)]
    if dec is not None:
        d_in, d_out, d_shape, d_scratch = _dec_specs(dec, lambda b, h, i: (b * heads + h) * nq + i)
        in_specs += d_in
        out_specs.append(d_out)
        out_shape.append(d_shape)
        scratch += d_scratch
    return pl.pallas_call(
        functools.partial(_fox_kernel, t=t, dec=dec),
        grid_spec=pltpu.PrefetchScalarGridSpec(
            num_scalar_prefetch=1, grid=(batch, heads, nq), in_specs=in_specs, out_specs=out_specs,
            scratch_shapes=scratch),
        out_shape=out_shape,
        compiler_params=_cparams(*(["parallel" if dec is None else "arbitrary"] * 2), "arbitrary"),
        name="fox_prompt",
    )(page_table, p["q_aug"], p["k_aug"], p["v_t"], p["sg_f"], *dec_ops)


def _page_bias_kernel(x_ref, u_ref, o_ref, *, heads):
    x = x_ref[...]
    n = x.shape[1]
    o_ref[:, :n] = _dot01_nt(x, u_ref[...])
    t = x[:, :LANES]
    for i in range(1, n // LANES):
        t = t + x[:, i * LANES:(i + 1) * LANES]
    shift = heads
    while shift < LANES:
        t = t + pltpu.roll(t, shift, 1)
        shift *= 2
    o_ref[:, n:] = jnp.concatenate([t] * (n // LANES), axis=1)


def _page_bias_call(lf_flat, heads):
    n_pages, n = lf_flat.shape
    assert LANES % heads == 0 and n % LANES == 0
    idx = jnp.arange(n, dtype=jnp.int32)
    later = (idx[:, None] // heads) > (idx[None, :] // heads)
    same_head = (idx[:, None] % heads) == (idx[None, :] % heads)
    u = (later & same_head).astype(BF16)
    pb = _largest_tile(n_pages, BIAS_PAGE_TILE, SUBLANES)
    return pl.pallas_call(
        functools.partial(_page_bias_kernel, heads=heads),
        grid=(n_pages // pb,),
        in_specs=[pl.BlockSpec((pb, n), lambda i: (i, 0)), pl.BlockSpec((n, n), lambda i: (0, 0))],
        out_specs=pl.BlockSpec((pb, 2 * n), lambda i: (i, 0)),
        out_shape=jax.ShapeDtypeStruct((n_pages, 2 * n), F32),
        compiler_params=_cparams("parallel"),
        name="page_bias",
    )(lf_flat, u)


class _Dec(NamedTuple):
    layer: int
    seq0: int
    nseq: int
    n_pg: int
    pps: int
    rows: int
    heads: int
    hd: int
    steps: int

    @property
    def groups(self):
        return self.n_pg // self.pps

    @property
    def guarded(self):
        return self.steps != self.nseq * self.groups


def _dec_where(dec, lin):
    active = lin < dec.nseq * dec.groups
    local = jnp.minimum(lin // dec.groups, dec.nseq - 1)
    grp = jnp.where(active, lin % dec.groups, dec.groups - 1)
    return active, local, grp


def _dec_num_inputs(dec):
    return 2 + 3 * dec.pps


_SEQ_Q, _SEQ_K, _SEQ_V, _SEQ_GATE = range(4)


def _dec_specs(dec, lin_of):
    heads, hd = dec.heads, dec.hd
    n = dec.rows * heads

    def seq_blk(shape, base):
        def imap(*a):
            return (base + _dec_where(dec, lin_of(*a[:-1]))[1], 0, 0)
        return pl.BlockSpec((None,) + shape, imap)

    def page(a, slot):
        _, local, grp = _dec_where(dec, lin_of(*a[:-1]))
        return a[-1][dec.seq0 + local, dec.n_pg - 1 - (grp * dec.pps + slot)]

    def cache_blk(slot):
        return pl.BlockSpec((None, None, dec.rows, heads, hd),
                            lambda *a: (dec.layer, page(a, slot), 0, 0, 0))

    def bias_blk(slot):
        return pl.BlockSpec((SUBLANES, 2 * n), lambda *a: (page(a, slot) // SUBLANES, 0))

    slots = range(dec.pps)
    in_specs = ([seq_blk((4 * HEAD_PAD, hd), dec.seq0), seq_blk((1, n), dec.seq0)]
                + [cache_blk(s) for s in slots] + [cache_blk(s) for s in slots]
                + [bias_blk(s) for s in slots])
    scratch = [pltpu.VMEM((HEAD_PAD, LANES), F32), pltpu.VMEM((HEAD_PAD, LANES), F32),
               pltpu.VMEM((1, n), F32), pltpu.VMEM((HEAD_PAD, hd), F32)]
    out_shape = jax.ShapeDtypeStruct((dec.nseq, HEAD_PAD, hd), BF16)
    return in_specs, seq_blk((HEAD_PAD, hd), 0), out_shape, scratch


def _seq_rows(seq_ref, which):
    return seq_ref[which * HEAD_PAD:(which + 1) * HEAD_PAD, :]


def _dec_pages(dec, pt_ref, local, grp, in_refs, scratch):
    pps, heads, hd = dec.pps, dec.heads, dec.hd
    seq_ref, cn_ref = in_refs[:2]
    k_refs = in_refs[2:2 + pps]
    v_refs = in_refs[2 + pps:2 + 2 * pps]
    b_refs = in_refs[2 + 2 * pps:2 + 3 * pps]
    m_ref, l_ref, run_ref, acc_ref = scratch
    n = dec.rows * heads

    @pl.when(grp == 0)
    def _():
        m_ref[...] = jnp.full_like(m_ref, -jnp.inf)
        l_ref[...] = jnp.zeros_like(l_ref)
        run_ref[...] = jnp.zeros_like(run_ref)
        acc_ref[...] = jnp.zeros_like(acc_ref)

    q = _seq_rows(seq_ref, _SEQ_Q).astype(BF16)
    sub = lax.broadcasted_iota(jnp.int32, (HEAD_PAD, n), 0)
    lane = lax.broadcasted_iota(jnp.int32, (HEAD_PAD, n), 1)
    own = (lane % heads) == (sub % heads)
    cn = cn_ref[...]
    run = run_ref[...]
    scores = []
    for p in range(pps):
        r = pt_ref[dec.seq0 + local, dec.n_pg - 1 - (grp * pps + p)] % SUBLANES
        kb = k_refs[p][...].reshape(n, hd).astype(BF16)
        bias = (b_refs[p][pl.ds(r, 1), :n] + run + cn) * LOG2E
        scores.append(jnp.where(own, _dot_nt(q, kb) + bias, -jnp.inf))
        run = run + b_refs[p][pl.ds(r, 1), n:]
    run_ref[...] = run

    m_prev = m_ref[:, :1]
    m_new = m_prev
    for s in scores:
        m_new = jnp.maximum(m_new, jnp.max(s, axis=-1, keepdims=True))
    alpha = jnp.exp2(m_prev - m_new)
    l_new = alpha * l_ref[:, :1]
    acc = alpha * acc_ref[...]
    for p in range(pps):
        pr = jnp.exp2(scores[p] - m_new)
        l_new = l_new + jnp.sum(pr, axis=-1, keepdims=True)
        acc = acc + _dot(pr.astype(BF16), v_refs[p][...].reshape(n, hd).astype(BF16))
    m_ref[...] = jnp.broadcast_to(m_new, m_ref.shape)
    l_ref[...] = jnp.broadcast_to(l_new, l_ref.shape)
    acc_ref[...] = acc
    return m_new, l_new, acc


def _dec_new_row(dec, grp, in_refs, o_ref, state):
    seq_ref = in_refs[0]
    m_new, l_new, acc = state

    @pl.when(grp == dec.groups - 1)
    def _():
        s_new = jnp.sum(_seq_rows(seq_ref, _SEQ_Q) * _seq_rows(seq_ref, _SEQ_K), axis=-1,
                        keepdims=True)
        m_fin = jnp.maximum(m_new, s_new)
        a_fin = jnp.exp2(m_new - m_fin)
        p_new = jnp.exp2(s_new - m_fin)
        l_fin = a_fin * l_new + p_new
        out = (a_fin * acc + p_new * _seq_rows(seq_ref, _SEQ_V)) / l_fin
        o_ref[...] = (out * _seq_rows(seq_ref, _SEQ_GATE)).astype(BF16)


def _dec_main(dec, pt_ref, lin, in_refs, o_ref, scratch):
    active, local, grp = _dec_where(dec, lin)
    if not dec.guarded:
        return _dec_pages(dec, pt_ref, local, grp, in_refs, scratch)

    @pl.when(active)
    def _():
        state = _dec_pages(dec, pt_ref, local, grp, in_refs, scratch)
        _dec_new_row(dec, grp, in_refs, o_ref, state)
    return None


def _dec_finish(dec, pt_ref, lin, in_refs, o_ref, scratch, state):
    del pt_ref, scratch
    if state is not None:
        _dec_new_row(dec, _dec_where(dec, lin)[2], in_refs, o_ref, state)


def _decode_kernel(pt_ref, *refs, dec):
    n_in = _dec_num_inputs(dec)
    args = (dec, pt_ref, pl.program_id(0), refs[:n_in], refs[n_in], refs[n_in + 1:])
    _dec_finish(*args, _dec_main(*args))


def _decode_operands(p, cache_k, cache_v, page_bias, pps, heads, hd):
    db = p["f_logf"].shape[0]
    rows = cache_k.shape[2]
    padh = lambda x: jnp.pad(x.astype(F32), ((0, 0), (0, HEAD_PAD - heads), (0, 0)))
    head_rows = lambda x: padh(x.reshape(db, heads, hd))
    main = lambda x: padh(x.reshape(db, heads, 2 * hd)[:, :, :hd])
    seq = jnp.concatenate([main(p["q_aug"]), main(p["k_aug"]), head_rows(p["f_v"]),
                           head_rows(p["sg_f"])], axis=1)
    cn = jnp.tile(p["f_logf"], (1, rows)).reshape(db, 1, rows * heads)
    return [seq, cn] + [cache_k] * pps + [cache_v] * pps + [page_bias] * pps


def _decode_call(dec, dec_ops, page_table):
    in_specs, out_spec, out_shape, scratch = _dec_specs(dec, lambda i: i)
    return pl.pallas_call(
        functools.partial(_decode_kernel, dec=dec),
        grid_spec=pltpu.PrefetchScalarGridSpec(
            num_scalar_prefetch=1, grid=(dec.steps,), in_specs=in_specs, out_specs=out_spec,
            scratch_shapes=scratch),
        out_shape=out_shape,
        compiler_params=_cparams("arbitrary"),
        name="fox_decode",
    )(page_table, *dec_ops)


def _out_kernel(a_ref, b_ref, wa_ref, wb_ref, x_ref, gate_ref, y_ref):
    out = _dot(a_ref[...], wa_ref[...]) + _dot(b_ref[...], wb_ref[...])
    y_ref[...] = x_ref[...] + gate_ref[...] * out


def _out_call(a, b, w_out_bf, x2, mod3, tm, tiles_per_group):
    m, d = x2.shape
    da, db_ = a.shape[1], b.shape[1]
    assert da == db_, "w_out is read as two equal row blocks"
    return pl.pallas_call(
        _out_kernel,
        grid=(m // tm,),
        in_specs=[pl.BlockSpec((tm, da), lambda i: (i, 0)),
                  pl.BlockSpec((tm, db_), lambda i: (i, 0)),
                  pl.BlockSpec((da, d), lambda i: (0, 0)),
                  pl.BlockSpec((db_, d), lambda i: (1, 0)),
                  pl.BlockSpec((tm, d), lambda i: (i, 0)),
                  _mod_spec(mod3, d, 2, tiles_per_group)],
        out_specs=pl.BlockSpec((tm, d), lambda i: (i, 0)),
        out_shape=jax.ShapeDtypeStruct((m, d), F32),
        compiler_params=_cparams("parallel"),
        name="out_proj",
    )(a, b, w_out_bf, w_out_bf, x2, mod3)


def kernel(x_prompt, x_sample, c_prompt, c_sample, cache_k, cache_v, cache_logf, state_hgrn,
           page_table, norm_w, w_ada, b_ada, w_in, b_fox_f, lb_logits, q_norm_w, k_norm_w,
           hgrn_norm_w, w_out):
    batch, seq, d = x_prompt.shape
    db, dseq, _ = x_sample.shape
    assert dseq == 1, "the sample group is one new row per sequence"
    depth, n_phys, page_rows, fheads, hd = cache_k.shape
    _, _, hheads, dk, dv = state_hgrn.shape
    hk, dh, dfox = hheads * dk, hheads * dv, fheads * hd
    n_seg = 2 * hk + 2 * dh + 4 * dfox
    assert w_in.shape[2] == n_seg + fheads
    mp, ms = batch * seq, db * dseq
    tm = _largest_tile(seq, ROW_TILE, LANES)

    xp = x_prompt.reshape(mp, d)
    xs = x_sample.reshape(ms, d)
    r_pad = -(batch + db) % SUBLANES
    c_all = jnp.concatenate([c_prompt, c_sample, jnp.zeros((r_pad, d), F32)], axis=0)

    outs = [[] for _ in range(8)]
    for l in range(depth):
        mod = _ada_call(c_all, w_ada[l], b_ada[l])
        mod_p = mod[:batch].reshape(batch, 1, 3 * d)
        mod_s = mod[batch:batch + db].reshape(1, db, 3 * d)
        w_bf = w_in[l].astype(BF16)
        w_ff = jnp.pad(w_bf[:, n_seg:], ((0, 0), (0, LANES - fheads)))
        b_ff = jnp.pad(b_fox_f[l].astype(F32), (0, LANES - fheads)).reshape(1, LANES)
        w_out_bf = w_out[l].astype(BF16)
        proj = functools.partial(_in_proj, w_bf=w_bf, w_ff=w_ff, b_ff=b_ff, lb_logits=lb_logits,
                                 q_norm_w=q_norm_w[l], k_norm_w=k_norm_w[l], layer=l,
                                 hgrn_heads=hheads, dk=dk, fox_heads=fheads, hd=hd)

        hp = _norm_call(xp, norm_w[l], mod_p, tm, seq // tm)
        pp = proj(hp, tm=tm, seq_len=seq, with_vt=True)
        hs = _norm_call(xs, norm_w[l], mod_s, ms, 1)
        ps = proj(hs, tm=ms, seq_len=1)

        page_bias = _page_bias_call(cache_logf[l].reshape(n_phys, page_rows * fheads), fheads)
        n_pg = page_table.shape[1]
        pps = _largest_tile(n_pg, PAGES_PER_STEP, 1)
        dec_ops = _decode_operands(ps, cache_k, cache_v, page_bias, pps, fheads, hd)
        fox_steps, hgrn_steps = _fox_steps(batch, seq, fheads, tm), _hgrn_steps(batch, seq)
        n_fox = min(db - db // 2, fox_steps // (n_pg // pps))
        n_hgrn = min(db - n_fox, hgrn_steps // (n_pg // pps))
        n_rest = db - n_fox - n_hgrn
        plan = lambda seq0, nseq, steps: None if nseq == 0 else _Dec(
            layer=l, seq0=seq0, nseq=nseq, n_pg=n_pg, pps=pps, rows=page_rows, heads=fheads, hd=hd,
            steps=steps)
        dec_fox = plan(0, n_fox, fox_steps)
        dec_hgrn = plan(n_fox, n_hgrn, hgrn_steps)
        dec_rest = plan(n_fox + n_hgrn, n_rest, n_rest * (n_pg // pps))
        hosted = lambda dec: () if dec is None else tuple(dec_ops)

        a_p, s_p, *o_hgrn = _hgrn_prompt_call(pp, hgrn_norm_w[l], batch, seq, hheads, dk, dv,
                                              page_table, dec_hgrn, hosted(dec_hgrn))
        b_p, *o_fox = _fox_prompt_call(pp, batch, seq, fheads, hd, tm, page_table, dec_fox,
                                       hosted(dec_fox))
        xp = _out_call(a_p, b_p, w_out_bf, xp, mod_p, tm, seq // tm)
        outs[0].append(pp["f_k"].reshape(batch, seq, fheads, hd))
        outs[1].append(pp["f_v"].reshape(batch, seq, fheads, hd))
        outs[2].append(pp["f_logf"].reshape(batch, seq, fheads))
        outs[3].append(s_p)

        a_s, s_s = _hgrn_step_call(ps, hgrn_norm_w[l], state_hgrn[l], hheads, dk, dv)
        o_rest = [] if dec_rest is None else [_decode_call(dec_rest, dec_ops, page_table)]
        b_s = jnp.concatenate(o_fox + o_hgrn + o_rest, axis=0)[:, :fheads, :].reshape(db, dfox)
        xs = _out_call(a_s, b_s, w_out_bf, xs, mod_s, ms, 1)
        outs[4].append(ps["f_k"].reshape(db, dseq, fheads, hd))
        outs[5].append(ps["f_v"].reshape(db, dseq, fheads, hd))
        outs[6].append(ps["f_logf"].reshape(db, dseq, fheads))
        outs[7].append(s_s)

    stk = [jnp.stack(o) for o in outs]
    return (xp.reshape(batch, seq, d), xs.reshape(db, dseq, d), *stk)
```

```python
import functools
from typing import NamedTuple

import jax
import jax.numpy as jnp
from jax import lax
from jax.experimental import pallas as pl
from jax.experimental.pallas import tpu as pltpu

F32 = jnp.float32
BF16 = jnp.bfloat16

LANES = 128
SUBLANES = 8
VMEM_LIMIT = 56 * 1024 * 1024
RMS_EPS = 1e-6
LOG2E = 1.4426950408889634
HGRN_CHUNK = 128
ROW_TILE = 512
PAGES_PER_STEP = 8
HGRN_PAGES_PER_STEP = 16
BIAS_PAGE_TILE = 256
HEAD_PAD = 16


def _cparams(*sem):
    return pltpu.CompilerParams(dimension_semantics=sem, vmem_limit_bytes=VMEM_LIMIT)


def _dot(a, b):
    return jnp.dot(a, b, preferred_element_type=F32)


def _dot_nt(a, b):
    return lax.dot_general(a, b, (((1,), (1,)), ((), ())), preferred_element_type=F32)


def _dot_tn(a, b):
    return lax.dot_general(a, b, (((0,), (0,)), ((), ())), preferred_element_type=F32)


def _split3(x):
    hi = x.astype(BF16)
    r = x - hi.astype(F32)
    mid = r.astype(BF16)
    lo = (r - mid.astype(F32)).astype(BF16)
    return hi, mid, lo


def _dot01(m01, x):
    hi, mid, lo = _split3(x)
    return _dot(m01, hi) + _dot(m01, mid) + _dot(m01, lo)


def _dot01_nt(x, m01):
    hi, mid, lo = _split3(x)
    return _dot(hi, m01) + _dot(mid, m01) + _dot(lo, m01)


def _sigmoid(x):
    return 1.0 / (1.0 + jnp.exp(-x))


def _silu(x):
    return x * _sigmoid(x)


def _rms(x, w):
    ms = jnp.mean(x * x, axis=-1, keepdims=True)
    return x * lax.rsqrt(ms + RMS_EPS) * w


def _largest_tile(n, cap, align):
    for t in range(min(cap, n), 0, -1):
        if n % t == 0 and t % align == 0:
            return t
    return n


def _ada_kernel(c_ref, w_ref, b_ref, o_ref):
    s = _silu(c_ref[...])
    s_hi = s.astype(BF16)
    s_lo = (s - s_hi.astype(F32)).astype(BF16)
    w = w_ref[...]
    w_hi = w.astype(BF16)
    w_lo = (w - w_hi.astype(F32)).astype(BF16)
    o_ref[...] = _dot(s_hi, w_hi) + _dot(s_hi, w_lo) + _dot(s_lo, w_hi) + b_ref[...]


def _ada_call(c_all, w_ada, b_ada):
    r, d = c_all.shape
    n = w_ada.shape[1]
    tn = 512
    return pl.pallas_call(
        _ada_kernel,
        grid=(n // tn,),
        in_specs=[pl.BlockSpec((r, d), lambda j: (0, 0)),
                  pl.BlockSpec((d, tn), lambda j: (0, j)),
                  pl.BlockSpec((1, tn), lambda j: (0, j))],
        out_specs=pl.BlockSpec((r, tn), lambda j: (0, j)),
        out_shape=jax.ShapeDtypeStruct((r, n), F32),
        compiler_params=_cparams("parallel"),
        name="ada_mod",
    )(c_all, w_ada, b_ada.reshape(1, n))


def _mod_spec(mod3, d, col, tiles_per_group):
    return pl.BlockSpec((None, mod3.shape[1], d), lambda i: (i // tiles_per_group, 0, col))


def _proj_first_kernel(x_ref, nw_ref, shift_ref, scale_ref, w_ref, wf_ref, bf_ref,
                       h_ref, hq_ref, logf_ref, c_ref, carry_ref, *, heads, tiles_per_seq):
    xn = _rms(x_ref[...], nw_ref[...])
    h = (xn * (1.0 + scale_ref[...]) + shift_ref[...]).astype(BF16)
    h_ref[...] = h
    hq_ref[...] = _dot(h, w_ref[...]).astype(BF16)
    x = _dot(h, wf_ref[...]) + bf_ref[...]
    lf = jnp.minimum(x, 0.0) - jnp.log1p(jnp.exp(-jnp.abs(x)))
    logf_ref[...] = lf[:, :heads]
    if tiles_per_seq is None:
        c_ref[...] = lf[:, :heads]
        return
    tm = lf.shape[0]

    @pl.when(pl.program_id(0) % tiles_per_seq == 0)
    def _():
        carry_ref[...] = jnp.zeros_like(carry_ref)

    row = lax.broadcasted_iota(jnp.int32, (tm, tm), 0)
    col = lax.broadcasted_iota(jnp.int32, (tm, tm), 1)
    tril = jnp.where(col <= row, 1.0, 0.0).astype(BF16)
    c = _dot01(tril, lf) + carry_ref[...]
    c_ref[...] = c[:, :heads]
    carry_ref[...] = c[tm - 1:tm, :]


def _proj_first_call(x2, norm_w, mod3, tiles_per_group, w_bf, seg_w, w_ff, b_ff, heads, tm, seq_len):
    m, d = x2.shape
    row = lambda width: pl.BlockSpec((tm, width), lambda i: (i, 0))
    tiles_per_seq = None if seq_len == 1 else seq_len // tm
    return pl.pallas_call(
        functools.partial(_proj_first_kernel, heads=heads, tiles_per_seq=tiles_per_seq),
        grid=(m // tm,),
        in_specs=[row(d), pl.BlockSpec((1, d), lambda i: (0, 0)),
                  _mod_spec(mod3, d, 0, tiles_per_group), _mod_spec(mod3, d, 1, tiles_per_group),
                  pl.BlockSpec((d, seg_w), lambda i: (0, 0)),
                  pl.BlockSpec(w_ff.shape, lambda i: (0, 0)), pl.BlockSpec(b_ff.shape, lambda i: (0, 0))],
        out_specs=(row(d), row(seg_w), row(heads), row(heads)),
        out_shape=(jax.ShapeDtypeStruct((m, d), BF16), jax.ShapeDtypeStruct((m, seg_w), BF16),
                   jax.ShapeDtypeStruct((m, heads), F32), jax.ShapeDtypeStruct((m, heads), F32)),
        scratch_shapes=[pltpu.VMEM((1, LANES), F32)],
        compiler_params=_cparams("arbitrary"),
        name="proj_first",
    )(x2, norm_w.reshape(1, d), mod3, mod3, w_bf, w_ff, b_ff)


def _proj_plain_kernel(h_ref, w_ref, o_ref, *, act):
    z = _dot(h_ref[...], w_ref[...])
    if act:
        z = _silu(z)
    o_ref[...] = z.astype(o_ref.dtype)


def _proj_v_kernel(h_ref, w_ref, o_ref):
    o_ref[...] = _dot(h_ref[...], w_ref[...])


def _proj_vt_kernel(h_ref, w_ref, o_ref, ot_ref):
    z = _dot(h_ref[...], w_ref[...])
    o_ref[...] = z
    ot_ref[...] = z.T.astype(BF16)


def _proj_forget_kernel(h_ref, w_ref, lbl_ref, logf_ref, k_ref, *, layer):
    z = _dot(h_ref[...], w_ref[...])
    lbl = lbl_ref[...]
    e = jnp.exp(lbl - jnp.max(lbl, axis=0, keepdims=True))
    lb = jnp.sum(e[:layer + 1], axis=0, keepdims=True) / jnp.sum(e, axis=0, keepdims=True)
    t = jnp.exp(-jnp.abs(z))
    big = 1.0 / (1.0 + t)
    small = t * big
    pos = z >= 0.0
    logf_ref[...] = jnp.log(lb + (1.0 - lb) * jnp.where(pos, big, small))
    k_ref[...] = ((1.0 - lb) * jnp.where(pos, small, big)).astype(BF16)


def _aug_lanes(c2, is_query, tm):
    hi, mid, lo = _split3(c2)
    hi, mid, lo = hi.astype(F32), mid.astype(F32), lo.astype(F32)
    lane = lax.broadcasted_iota(jnp.int32, (tm, LANES), 1)
    one = jnp.ones((tm, LANES), F32)
    zero = jnp.zeros((tm, LANES), F32)
    if is_query:
        parts = jnp.where(lane == 0, hi, jnp.where(lane == 1, mid, jnp.where(lane == 2, lo, zero)))
        return jnp.where((lane >= 3) & (lane < 6), one, parts).astype(BF16)
    parts = jnp.where(lane == 3, -hi, jnp.where(lane == 4, -mid, jnp.where(lane == 5, -lo, zero)))
    return jnp.where(lane < 3, one, parts).astype(BF16)


def _proj_qk_kernel(h_ref, w_ref, nw_ref, c_ref, *out_refs, heads, hd, is_query, qscale):
    z = _dot(h_ref[...], w_ref[...])
    tm = z.shape[0]
    nw = nw_ref[...]
    c = c_ref[...]
    for h in range(heads):
        n = _rms(z[:, h * hd:(h + 1) * hd], nw)
        aug = _aug_lanes(c[:, h:h + 1] * LOG2E, is_query, tm)
        if is_query:
            (aug_ref,) = out_refs
            n = n * qscale
        else:
            full_ref, aug_ref = out_refs
            full_ref[:, h * hd:(h + 1) * hd] = n
        aug_ref[:, 2 * h * hd:(2 * h + 1) * hd] = n.astype(BF16)
        aug_ref[:, (2 * h + 1) * hd:(2 * h + 2) * hd] = aug


def _proj_call(kernel, h, w, seg_w, seg, extra_inputs, extra_specs, out_shapes, out_specs, tm,
               scratch=(), sem="parallel", name="in_proj"):
    m, d = h.shape
    return pl.pallas_call(
        kernel,
        grid=(m // tm,),
        in_specs=[pl.BlockSpec((tm, d), lambda i: (i, 0)),
                  pl.BlockSpec((d, seg_w), lambda i: (0, seg))] + list(extra_specs),
        out_specs=out_specs,
        out_shape=out_shapes,
        scratch_shapes=list(scratch),
        compiler_params=_cparams(sem),
        name=name,
    )(h, w, *extra_inputs)


def _full_spec(shape):
    nd = len(shape)
    return pl.BlockSpec(shape, lambda i: (0,) * nd)


def _in_proj(x2, norm_w, mod3, tiles_per_group, w_bf, w_ff, b_ff, lb_logits, q_norm_w, k_norm_w,
             layer, tm, seq_len, hgrn_heads, dk, fox_heads, hd, with_vt=False):
    m, d = x2.shape
    hk = hgrn_heads * dk
    dfox = fox_heads * hd
    row = lambda width: pl.BlockSpec((tm, width), lambda i: (i, 0))
    sds = lambda width, dt: jax.ShapeDtypeStruct((m, width), dt)
    assert hk == dfox, "segments are indexed as equal-width column blocks"
    h, hq, f_logf, f_c = _proj_first_call(x2, norm_w, mod3, tiles_per_group, w_bf, hk, w_ff, b_ff,
                                          fox_heads, tm, seq_len)
    plain = lambda seg, act, nm: _proj_call(
        functools.partial(_proj_plain_kernel, act=act), h, w_bf, hk, seg, (), (),
        sds(hk, BF16), row(hk), tm, name=nm)
    a_logf, a_k = _proj_call(
        functools.partial(_proj_forget_kernel, layer=layer), h, w_bf, hk, 1,
        (lb_logits,), (_full_spec(lb_logits.shape),),
        (sds(hk, F32), sds(hk, BF16)), (row(hk), row(hk)), tm, name="proj_hf")
    hi = plain(2, False, "proj_hi")
    sg_h = plain(3, True, "proj_hg")
    qk = lambda seg, nw, is_query, outs, specs, nm: _proj_call(
        functools.partial(_proj_qk_kernel, heads=fox_heads, hd=hd, is_query=is_query,
                          qscale=hd ** -0.5 * LOG2E),
        h, w_bf, dfox, seg, (nw.reshape(1, hd), f_c), (_full_spec((1, hd)), row(fox_heads)),
        outs, specs, tm, name=nm)
    q_aug = qk(4, q_norm_w, True, sds(2 * dfox, BF16), row(2 * dfox), "proj_fq")
    f_k, k_aug = qk(5, k_norm_w, False, (sds(dfox, F32), sds(2 * dfox, BF16)),
                    (row(dfox), row(2 * dfox)), "proj_fk")
    if not with_vt:
        f_v = _proj_call(_proj_v_kernel, h, w_bf, dfox, 6, (), (), sds(dfox, F32), row(dfox), tm,
                         name="proj_fv")
        v_t = None
    else:
        f_v, v_t = _proj_call(
            _proj_vt_kernel, h, w_bf, dfox, 6, (), (),
            (sds(dfox, F32), jax.ShapeDtypeStruct((m // tm, dfox, tm), BF16)),
            (row(dfox), pl.BlockSpec((None, dfox, tm), lambda i: (i, 0, 0))), tm, name="proj_fv")
    sg_f = plain(7, True, "proj_fg")
    return dict(hq=hq, a_logf=a_logf, a_k=a_k, hi=hi, sg_h=sg_h, f_logf=f_logf, f_c=f_c,
                q_aug=q_aug, f_k=f_k, k_aug=k_aug, f_v=f_v, v_t=v_t, sg_f=sg_f)


def _hgrn_levels(c):
    levels = []
    m = c // 2
    while m >= 1:
        levels.append(m)
        m //= 2
    return levels


def _hgrn_level_exponents(b, g, c):
    dk = b.shape[1]
    ridx = lax.broadcasted_iota(jnp.int32, (c, dk), 0)
    out = []
    for m in _hgrn_levels(c):
        pos = ridx & (2 * m - 1)
        later = pos >= m
        if 2 * m >= SUBLANES:
            b3 = b.reshape(c // (2 * m), 2 * m, dk)
            bm = jnp.broadcast_to(b3[:, m - 1:m, :], b3.shape).reshape(c, dk)
            e = jnp.where(later, b - bm, bm - b)
        elif m == 2:
            g_next = pltpu.roll(g, c - 1, 0)
            g_prev = pltpu.roll(g, 1, 0)
            e = jnp.where(pos == 0, g_next,
                          jnp.where(pos == 2, g, jnp.where(pos == 3, g + g_prev, 0.0)))
        else:
            e = jnp.where(later, g, 0.0)
        out.append((e, later))
    return out


def _hgrn_kernel(pt_ref, q_ref, k_ref, g_ref, v_ref, sg_ref, nw_ref, *rest, c, heads, dk, dv, dec):
    ci = pl.program_id(1)
    if dec is None:
        o_ref, s_ref, st_ref = rest
    else:
        n_in = _dec_num_inputs(dec)
        (o_ref, s_ref, do_ref), st_ref = rest[n_in:n_in + 3], rest[n_in + 3]
        lin = pl.program_id(0) * pl.num_programs(1) + ci
        dec_args = (dec, pt_ref, lin, rest[:n_in], do_ref, rest[n_in + 4:])

    @pl.when(ci == 0)
    def _():
        st_ref[...] = jnp.zeros_like(st_ref)

    dec_state = None if dec is None else _dec_main(*dec_args)

    row = lax.broadcasted_iota(jnp.int32, (c, c), 0)
    col = lax.broadcasted_iota(jnp.int32, (c, c), 1)
    tril = jnp.where(col <= row, 1.0, 0.0).astype(BF16)
    eye = row == col
    masks = []
    for m in _hgrn_levels(c):
        same = (row // (2 * m)) == (col // (2 * m))
        masks.append(same & ((row & (2 * m - 1)) >= m) & ((col & (2 * m - 1)) < m))
    nw = nw_ref[...]

    for h in range(heads):
        ks = slice(h * dk, (h + 1) * dk)
        vs = slice(h * dv, (h + 1) * dv)
        q = q_ref[:, ks].astype(F32)
        k = k_ref[:, ks].astype(F32)
        g = g_ref[:, ks] * LOG2E
        v = v_ref[:, vs]
        b = _dot01(tril, g)
        b_last = b[c - 1:c]
        a = jnp.where(eye, _dot_nt(q.astype(BF16), k.astype(BF16)), 0.0)
        for (e, later), mask in zip(_hgrn_level_exponents(b, g, c), masks):
            x = (jnp.where(later, q, k) * jnp.exp2(e)).astype(BF16)
            a = jnp.where(mask, _dot_nt(x, x), a)
        st = st_ref[h]
        o = _dot_nt((q * jnp.exp2(b)).astype(BF16), st.astype(BF16)) + _dot(a.astype(BF16), v)
        kd = (k * jnp.exp2(b_last - b)).astype(BF16)
        st_ref[h] = st * jnp.exp2(b_last) + _dot_tn(v, kd)
        o_ref[:, vs] = (_rms(o, nw) * sg_ref[:, vs].astype(F32)).astype(BF16)

    @pl.when(ci == pl.num_programs(1) - 1)
    def _():
        for h in range(heads):
            s_ref[h] = st_ref[h].T

    if dec is not None:
        _dec_finish(*dec_args, dec_state)


def _hgrn_steps(batch, seq):
    return batch * (seq // HGRN_CHUNK)


def _hgrn_prompt_call(p, hgrn_norm_w, batch, seq, heads, dk, dv, page_table, dec=None, dec_ops=()):
    c = HGRN_CHUNK
    nc = seq // c
    blk = lambda width: pl.BlockSpec((c, heads * width), lambda b, i, pt: (b * nc + i, 0))
    in_specs = [blk(dk), blk(dk), blk(dk), blk(dv), blk(dv),
                pl.BlockSpec((1, dv), lambda b, i, pt: (0, 0))]
    out_specs = [blk(dv), pl.BlockSpec((None, heads, dk, dv), lambda b, i, pt: (b, 0, 0, 0))]
    out_shape = [jax.ShapeDtypeStruct((batch * seq, heads * dv), BF16),
                 jax.ShapeDtypeStruct((batch, heads, dk, dv), F32)]
    scratch = [pltpu.VMEM((heads, dv, dk), F32)]
    if dec is not None:
        d_in, d_out, d_shape, d_scratch = _dec_specs(dec, lambda b, i: b * nc + i)
        in_specs += d_in
        out_specs.append(d_out)
        out_shape.append(d_shape)
        scratch += d_scratch
    return pl.pallas_call(
        functools.partial(_hgrn_kernel, c=c, heads=heads, dk=dk, dv=dv, dec=dec),
        grid_spec=pltpu.PrefetchScalarGridSpec(
            num_scalar_prefetch=1, grid=(batch, nc), in_specs=in_specs, out_specs=out_specs,
            scratch_shapes=scratch),
        out_shape=out_shape,
        compiler_params=_cparams("parallel" if dec is None else "arbitrary", "arbitrary"),
        name="hgrn_prompt",
    )(page_table, p["hq"], p["a_k"], p["a_logf"], p["hi"], p["sg_h"], hgrn_norm_w.reshape(1, dv),
      *dec_ops)


def _hgrn_step_kernel(q_ref, k_ref, g_ref, v_ref, sg_ref, nw_ref, s0_ref, o_ref, s_ref, *, heads):
    pad = lambda x: jnp.concatenate([x, jnp.zeros((LANES - heads, x.shape[1]), F32)], axis=0)
    qt = pad(q_ref[...].astype(F32)).T
    kt = pad(k_ref[...].astype(F32)).T
    ft = pad(jnp.exp(g_ref[...])).T
    v = v_ref[...].astype(F32)
    sg = sg_ref[...].astype(F32)
    rows = []
    for h in range(heads):
        s = ft[:, h:h + 1] * s0_ref[h] + kt[:, h:h + 1] * v[h:h + 1, :]
        s_ref[h] = s
        o = jnp.sum(qt[:, h:h + 1] * s, axis=0, keepdims=True)
        rows.append(_rms(o, nw_ref[...]) * sg[h:h + 1, :])
    o_ref[...] = jnp.concatenate(rows, axis=0).astype(BF16)


def _hgrn_step_call(p, hgrn_norm_w, state, heads, dk, dv):
    db = state.shape[0]
    r3 = lambda x, w: x.reshape(db, heads, w)
    blk = lambda w: pl.BlockSpec((None, heads, w), lambda b: (b, 0, 0))
    st_blk = pl.BlockSpec((None, heads, dk, dv), lambda b: (b, 0, 0, 0))
    o, s = pl.pallas_call(
        functools.partial(_hgrn_step_kernel, heads=heads),
        grid=(db,),
        in_specs=[blk(dk), blk(dk), blk(dk), blk(dv), blk(dv),
                  pl.BlockSpec((1, dv), lambda b: (0, 0)), st_blk],
        out_specs=(blk(dv), st_blk),
        out_shape=(jax.ShapeDtypeStruct((db, heads, dv), BF16),
                   jax.ShapeDtypeStruct(state.shape, F32)),
        compiler_params=_cparams("parallel"),
        name="hgrn_step",
    )(r3(p["hq"], dk), r3(p["a_k"], dk), r3(p["a_logf"], dk), r3(p["hi"], dv), r3(p["sg_h"], dv),
      hgrn_norm_w.reshape(1, dv), state)
    return o.reshape(db, heads * dv), s


def _fox_kernel(pt_ref, q_ref, k_ref, vt_ref, sg_ref, *rest, t, dec):
    qi = pl.program_id(2)
    if dec is None:
        o_ref, s_ref, m_ref, l_ref, acc_ref = rest
        dec_state = None
    else:
        n_in = _dec_num_inputs(dec)
        o_ref, do_ref, s_ref, m_ref, l_ref, acc_ref = rest[n_in:n_in + 6]
        lin = ((pl.program_id(0) * pl.num_programs(1) + pl.program_id(1)) * pl.num_programs(2) + qi)
        dec_args = (dec, pt_ref, lin, rest[:n_in], do_ref, rest[n_in + 6:])
        dec_state = _dec_main(*dec_args)
    q = q_ref[...]
    m_ref[...] = jnp.full_like(m_ref, -jnp.inf)
    l_ref[...] = jnp.zeros_like(l_ref)
    acc_ref[...] = jnp.zeros_like(acc_ref)

    def scores(j):
        return _dot_nt(k_ref[pl.ds(pl.multiple_of(j * t, t), t), :], q)

    def update(j, s):
        m_prev = m_ref[...]
        m_new = jnp.maximum(m_prev, jnp.max(s, axis=0, keepdims=True))
        alpha = jnp.exp2(m_prev - m_new)
        p = jnp.exp2(s - m_new)
        l_ref[...] = alpha * l_ref[...] + jnp.sum(p, axis=0, keepdims=True)
        acc_ref[...] = alpha * acc_ref[...] + _dot(vt_ref[j], p.astype(BF16))
        m_ref[...] = m_new

    s_ref[...] = scores(0)

    def body(j, carry):
        s = s_ref[...]
        s_next = scores(j + 1)
        update(j, s)
        s_ref[...] = s_next
        return carry

    lax.fori_loop(0, qi, body, 0)
    key = lax.broadcasted_iota(jnp.int32, (t, t), 0)
    qry = lax.broadcasted_iota(jnp.int32, (t, t), 1)
    update(qi, jnp.where(key <= qry, s_ref[...], -jnp.inf))
    o = (acc_ref[...] / l_ref[...]).T
    o_ref[...] = (o * sg_ref[...].astype(F32)).astype(BF16)
    if dec is not None:
        _dec_finish(*dec_args, dec_state)


def _fox_steps(batch, seq, heads, t):
    return batch * heads * (seq // t)


def _fox_prompt_call(p, batch, seq, heads, hd, t, page_table, dec=None, dec_ops=()):
    nq = seq // t
    in_specs = [pl.BlockSpec((t, 2 * hd), lambda b, h, i, pt: (b * nq + i, h)),
                pl.BlockSpec((seq, 2 * hd), lambda b, h, i, pt: (b, h)),
                pl.BlockSpec((nq, hd, t), lambda b, h, i, pt: (b, h, 0)),
                pl.BlockSpec((t, hd), lambda b, h, i, pt: (b * nq + i, h))]
    out_specs = [pl.BlockSpec((t, hd), lambda b, h, i, pt: (b * nq + i, h))]
    out_shape = [jax.ShapeDtypeStruct((batch * seq, heads * hd), BF16)]
    scratch = [pltpu.VMEM((t, t), F32), pltpu.VMEM((1, t), F32), pltpu.VMEM((1, t), F32),
               pltpu.VMEM((hd, t), F32)]
    if dec is not None:
        d_in, d_out, d_shape, d_scratch = _dec_specs(dec, lambda b, h, i: (b * heads + h) * nq + i)
        in_specs += d_in
        out_specs.append(d_out)
        out_shape.append(d_shape)
        scratch += d_scratch
    return pl.pallas_call(
        functools.partial(_fox_kernel, t=t, dec=dec),
        grid_spec=pltpu.PrefetchScalarGridSpec(
            num_scalar_prefetch=1, grid=(batch, heads, nq), in_specs=in_specs, out_specs=out_specs,
            scratch_shapes=scratch),
        out_shape=out_shape,
        compiler_params=_cparams(*(["parallel" if dec is None else "arbitrary"] * 2), "arbitrary"),
        name="fox_prompt",
    )(page_table, p["q_aug"], p["k_aug"], p["v_t"], p["sg_f"], *dec_ops)


def _page_bias_kernel(x_ref, u_ref, o_ref, *, heads):
    x = x_ref[...]
    n = x.shape[1]
    o_ref[:, :n] = _dot01_nt(x, u_ref[...])
    t = x[:, :LANES]
    for i in range(1, n // LANES):
        t = t + x[:, i * LANES:(i + 1) * LANES]
    shift = heads
    while shift < LANES:
        t = t + pltpu.roll(t, shift, 1)
        shift *= 2
    o_ref[:, n:] = jnp.concatenate([t] * (n // LANES), axis=1)


def _page_bias_call(lf_flat, heads):
    n_pages, n = lf_flat.shape
    assert LANES % heads == 0 and n % LANES == 0
    idx = jnp.arange(n, dtype=jnp.int32)
    later = (idx[:, None] // heads) > (idx[None, :] // heads)
    same_head = (idx[:, None] % heads) == (idx[None, :] % heads)
    u = (later & same_head).astype(BF16)
    pb = _largest_tile(n_pages, BIAS_PAGE_TILE, SUBLANES)
    return pl.pallas_call(
        functools.partial(_page_bias_kernel, heads=heads),
        grid=(n_pages // pb,),
        in_specs=[pl.BlockSpec((pb, n), lambda i: (i, 0)), pl.BlockSpec((n, n), lambda i: (0, 0))],
        out_specs=pl.BlockSpec((pb, 2 * n), lambda i: (i, 0)),
        out_shape=jax.ShapeDtypeStruct((n_pages, 2 * n), F32),
        compiler_params=_cparams("parallel"),
        name="page_bias",
    )(lf_flat, u)


class _Dec(NamedTuple):
    layer: int
    seq0: int
    nseq: int
    n_pg: int
    pps: int
    rows: int
    heads: int
    hd: int
    steps: int

    @property
    def groups(self):
        return self.n_pg // self.pps

    @property
    def guarded(self):
        return self.steps != self.nseq * self.groups


def _dec_where(dec, lin):
    active = lin < dec.nseq * dec.groups
    local = jnp.minimum(lin // dec.groups, dec.nseq - 1)
    grp = jnp.where(active, lin % dec.groups, dec.groups - 1)
    return active, local, grp


def _dec_num_inputs(dec):
    return 2 + 3 * dec.pps


_SEQ_Q, _SEQ_K, _SEQ_V, _SEQ_GATE = range(4)


def _dec_specs(dec, lin_of):
    heads, hd = dec.heads, dec.hd
    n = dec.rows * heads

    def seq_blk(shape, base):
        def imap(*a):
            return (base + _dec_where(dec, lin_of(*a[:-1]))[1], 0, 0)
        return pl.BlockSpec((None,) + shape, imap)

    def page(a, slot):
        _, local, grp = _dec_where(dec, lin_of(*a[:-1]))
        return a[-1][dec.seq0 + local, dec.n_pg - 1 - (grp * dec.pps + slot)]

    def cache_blk(slot):
        return pl.BlockSpec((None, None, dec.rows, heads, hd),
                            lambda *a: (dec.layer, page(a, slot), 0, 0, 0))

    def bias_blk(slot):
        return pl.BlockSpec((SUBLANES, 2 * n), lambda *a: (page(a, slot) // SUBLANES, 0))

    slots = range(dec.pps)
    in_specs = ([seq_blk((4 * HEAD_PAD, hd), dec.seq0), seq_blk((1, n), dec.seq0)]
                + [cache_blk(s) for s in slots] + [cache_blk(s) for s in slots]
                + [bias_blk(s) for s in slots])
    scratch = [pltpu.VMEM((HEAD_PAD, LANES), F32), pltpu.VMEM((HEAD_PAD, LANES), F32),
               pltpu.VMEM((1, n), F32), pltpu.VMEM((HEAD_PAD, hd), F32)]
    out_shape = jax.ShapeDtypeStruct((dec.nseq, HEAD_PAD, hd), BF16)
    return in_specs, seq_blk((HEAD_PAD, hd), 0), out_shape, scratch


def _seq_rows(seq_ref, which):
    return seq_ref[which * HEAD_PAD:(which + 1) * HEAD_PAD, :]


def _dec_pages(dec, pt_ref, local, grp, in_refs, scratch):
    pps, heads, hd = dec.pps, dec.heads, dec.hd
    seq_ref, cn_ref = in_refs[:2]
    k_refs = in_refs[2:2 + pps]
    v_refs = in_refs[2 + pps:2 + 2 * pps]
    b_refs = in_refs[2 + 2 * pps:2 + 3 * pps]
    m_ref, l_ref, run_ref, acc_ref = scratch
    n = dec.rows * heads

    @pl.when(grp == 0)
    def _():
        m_ref[...] = jnp.full_like(m_ref, -jnp.inf)
        l_ref[...] = jnp.zeros_like(l_ref)
        run_ref[...] = jnp.zeros_like(run_ref)
        acc_ref[...] = jnp.zeros_like(acc_ref)

    q = _seq_rows(seq_ref, _SEQ_Q).astype(BF16)
    sub = lax.broadcasted_iota(jnp.int32, (HEAD_PAD, n), 0)
    lane = lax.broadcasted_iota(jnp.int32, (HEAD_PAD, n), 1)
    own = (lane % heads) == (sub % heads)
    cn = cn_ref[...]
    run = run_ref[...]
    scores = []
    for p in range(pps):
        r = pt_ref[dec.seq0 + local, dec.n_pg - 1 - (grp * pps + p)] % SUBLANES
        kb = k_refs[p][...].reshape(n, hd).astype(BF16)
        bias = (b_refs[p][pl.ds(r, 1), :n] + run + cn) * LOG2E
        scores.append(jnp.where(own, _dot_nt(q, kb) + bias, -jnp.inf))
        run = run + b_refs[p][pl.ds(r, 1), n:]
    run_ref[...] = run

    m_prev = m_ref[:, :1]
    m_new = m_prev
    for s in scores:
        m_new = jnp.maximum(m_new, jnp.max(s, axis=-1, keepdims=True))
    alpha = jnp.exp2(m_prev - m_new)
    l_new = alpha * l_ref[:, :1]
    acc = alpha * acc_ref[...]
    for p in range(pps):
        pr = jnp.exp2(scores[p] - m_new)
        l_new = l_new + jnp.sum(pr, axis=-1, keepdims=True)
        acc = acc + _dot(pr.astype(BF16), v_refs[p][...].reshape(n, hd).astype(BF16))
    m_ref[...] = jnp.broadcast_to(m_new, m_ref.shape)
    l_ref[...] = jnp.broadcast_to(l_new, l_ref.shape)
    acc_ref[...] = acc
    return m_new, l_new, acc


def _dec_new_row(dec, grp, in_refs, o_ref, state):
    seq_ref = in_refs[0]
    m_new, l_new, acc = state

    @pl.when(grp == dec.groups - 1)
    def _():
        s_new = jnp.sum(_seq_rows(seq_ref, _SEQ_Q) * _seq_rows(seq_ref, _SEQ_K), axis=-1,
                        keepdims=True)
        m_fin = jnp.maximum(m_new, s_new)
        a_fin = jnp.exp2(m_new - m_fin)
        p_new = jnp.exp2(s_new - m_fin)
        l_fin = a_fin * l_new + p_new
        out = (a_fin * acc + p_new * _seq_rows(seq_ref, _SEQ_V)) / l_fin
        o_ref[...] = (out * _seq_rows(seq_ref, _SEQ_GATE)).astype(BF16)


def _dec_main(dec, pt_ref, lin, in_refs, o_ref, scratch):
    active, local, grp = _dec_where(dec, lin)
    if not dec.guarded:
        return _dec_pages(dec, pt_ref, local, grp, in_refs, scratch)

    @pl.when(active)
    def _():
        state = _dec_pages(dec, pt_ref, local, grp, in_refs, scratch)
        _dec_new_row(dec, grp, in_refs, o_ref, state)
    return None


def _dec_finish(dec, pt_ref, lin, in_refs, o_ref, scratch, state):
    del pt_ref, scratch
    if state is not None:
        _dec_new_row(dec, _dec_where(dec, lin)[2], in_refs, o_ref, state)


def _decode_kernel(pt_ref, *refs, dec):
    n_in = _dec_num_inputs(dec)
    args = (dec, pt_ref, pl.program_id(0), refs[:n_in], refs[n_in], refs[n_in + 1:])
    _dec_finish(*args, _dec_main(*args))


def _decode_operands(p, cache_k, cache_v, page_bias, pps, heads, hd):
    db = p["f_logf"].shape[0]
    rows = cache_k.shape[2]
    padh = lambda x: jnp.pad(x.astype(F32), ((0, 0), (0, HEAD_PAD - heads), (0, 0)))
    head_rows = lambda x: padh(x.reshape(db, heads, hd))
    main = lambda x: padh(x.reshape(db, heads, 2 * hd)[:, :, :hd])
    seq = jnp.concatenate([main(p["q_aug"]), main(p["k_aug"]), head_rows(p["f_v"]),
                           head_rows(p["sg_f"])], axis=1)
    cn = jnp.tile(p["f_logf"], (1, rows)).reshape(db, 1, rows * heads)
    return [seq, cn] + [cache_k] * pps + [cache_v] * pps + [page_bias] * pps


def _decode_call(dec, dec_ops, page_table):
    in_specs, out_spec, out_shape, scratch = _dec_specs(dec, lambda i: i)
    return pl.pallas_call(
        functools.partial(_decode_kernel, dec=dec),
        grid_spec=pltpu.PrefetchScalarGridSpec(
            num_scalar_prefetch=1, grid=(dec.steps,), in_specs=in_specs, out_specs=out_spec,
            scratch_shapes=scratch),
        out_shape=out_shape,
        compiler_params=_cparams("arbitrary"),
        name="fox_decode",
    )(page_table, *dec_ops)


def _out_kernel(a_ref, b_ref, wa_ref, wb_ref, x_ref, gate_ref, y_ref):
    out = _dot(a_ref[...], wa_ref[...]) + _dot(b_ref[...], wb_ref[...])
    y_ref[...] = x_ref[...] + gate_ref[...] * out


def _out_call(a, b, w_out_bf, x2, mod3, tm, tiles_per_group):
    m, d = x2.shape
    da, db_ = a.shape[1], b.shape[1]
    assert da == db_, "w_out is read as two equal row blocks"
    return pl.pallas_call(
        _out_kernel,
        grid=(m // tm,),
        in_specs=[pl.BlockSpec((tm, da), lambda i: (i, 0)),
                  pl.BlockSpec((tm, db_), lambda i: (i, 0)),
                  pl.BlockSpec((da, d), lambda i: (0, 0)),
                  pl.BlockSpec((db_, d), lambda i: (1, 0)),
                  pl.BlockSpec((tm, d), lambda i: (i, 0)),
                  _mod_spec(mod3, d, 2, tiles_per_group)],
        out_specs=pl.BlockSpec((tm, d), lambda i: (i, 0)),
        out_shape=jax.ShapeDtypeStruct((m, d), F32),
        compiler_params=_cparams("parallel"),
        name="out_proj",
    )(a, b, w_out_bf, w_out_bf, x2, mod3)


def kernel(x_prompt, x_sample, c_prompt, c_sample, cache_k, cache_v, cache_logf, state_hgrn,
           page_table, norm_w, w_ada, b_ada, w_in, b_fox_f, lb_logits, q_norm_w, k_norm_w,
           hgrn_norm_w, w_out):
    batch, seq, d = x_prompt.shape
    db, dseq, _ = x_sample.shape
    assert dseq == 1, "the sample group is one new row per sequence"
    depth, n_phys, page_rows, fheads, hd = cache_k.shape
    _, _, hheads, dk, dv = state_hgrn.shape
    hk, dh, dfox = hheads * dk, hheads * dv, fheads * hd
    n_seg = 2 * hk + 2 * dh + 4 * dfox
    assert w_in.shape[2] == n_seg + fheads
    mp, ms = batch * seq, db * dseq
    tm = _largest_tile(seq, ROW_TILE, LANES)

    xp = x_prompt.reshape(mp, d)
    xs = x_sample.reshape(ms, d)
    r_pad = -(batch + db) % SUBLANES
    c_all = jnp.concatenate([c_prompt, c_sample, jnp.zeros((r_pad, d), F32)], axis=0)

    outs = [[] for _ in range(8)]
    for l in range(depth):
        mod = _ada_call(c_all, w_ada[l], b_ada[l])
        mod_p = mod[:batch].reshape(batch, 1, 3 * d)
        mod_s = mod[batch:batch + db].reshape(1, db, 3 * d)
        w_bf = w_in[l].astype(BF16)
        w_ff = jnp.pad(w_bf[:, n_seg:], ((0, 0), (0, LANES - fheads)))
        b_ff = jnp.pad(b_fox_f[l].astype(F32), (0, LANES - fheads)).reshape(1, LANES)
        w_out_bf = w_out[l].astype(BF16)
        proj = functools.partial(_in_proj, norm_w=norm_w[l], w_bf=w_bf, w_ff=w_ff, b_ff=b_ff,
                                 lb_logits=lb_logits, q_norm_w=q_norm_w[l], k_norm_w=k_norm_w[l],
                                 layer=l, hgrn_heads=hheads, dk=dk, fox_heads=fheads, hd=hd)

        pp = proj(xp, mod3=mod_p, tiles_per_group=seq // tm, tm=tm, seq_len=seq, with_vt=True)
        ps = proj(xs, mod3=mod_s, tiles_per_group=1, tm=ms, seq_len=1)

        page_bias = _page_bias_call(cache_logf[l].reshape(n_phys, page_rows * fheads), fheads)
        n_pg = page_table.shape[1]
        pps = _largest_tile(n_pg, PAGES_PER_STEP, 1)
        pps_h = _largest_tile(n_pg, HGRN_PAGES_PER_STEP, 1)
        fox_steps, hgrn_steps = _fox_steps(batch, seq, fheads, tm), _hgrn_steps(batch, seq)
        n_fox = min(db - db // 2, fox_steps // (n_pg // pps))
        n_hgrn = min(db - n_fox, hgrn_steps // (n_pg // pps_h))
        n_rest = db - n_fox - n_hgrn
        plan = lambda seq0, nseq, steps, pages: None if nseq == 0 else _Dec(
            layer=l, seq0=seq0, nseq=nseq, n_pg=n_pg, pps=pages, rows=page_rows, heads=fheads,
            hd=hd, steps=steps)
        dec_fox = plan(0, n_fox, fox_steps, pps)
        dec_hgrn = plan(n_fox, n_hgrn, hgrn_steps, pps_h)
        dec_rest = plan(n_fox + n_hgrn, n_rest, n_rest * (n_pg // pps), pps)
        hosted = lambda dec: () if dec is None else tuple(
            _decode_operands(ps, cache_k, cache_v, page_bias, dec.pps, fheads, hd))

        a_p, s_p, *o_hgrn = _hgrn_prompt_call(pp, hgrn_norm_w[l], batch, seq, hheads, dk, dv,
                                              page_table, dec_hgrn, hosted(dec_hgrn))
        b_p, *o_fox = _fox_prompt_call(pp, batch, seq, fheads, hd, tm, page_table, dec_fox,
                                       hosted(dec_fox))
        xp = _out_call(a_p, b_p, w_out_bf, xp, mod_p, tm, seq // tm)
        outs[0].append(pp["f_k"].reshape(batch, seq, fheads, hd))
        outs[1].append(pp["f_v"].reshape(batch, seq, fheads, hd))
        outs[2].append(pp["f_logf"].reshape(batch, seq, fheads))
        outs[3].append(s_p)

        a_s, s_s = _hgrn_step_call(ps, hgrn_norm_w[l], state_hgrn[l], hheads, dk, dv)
        o_rest = [] if dec_rest is None else [_decode_call(dec_rest, hosted(dec_rest), page_table)]
        b_s = jnp.concatenate(o_fox + o_hgrn + o_rest, axis=0)[:, :fheads, :].reshape(db, dfox)
        xs = _out_call(a_s, b_s, w_out_bf, xs, mod_s, ms, 1)
        outs[4].append(ps["f_k"].reshape(db, dseq, fheads, hd))
        outs[5].append(ps["f_v"].reshape(db, dseq, fheads, hd))
        outs[6].append(ps["f_logf"].reshape(db, dseq, fheads))
        outs[7].append(s_s)

    stk = [jnp.stack(o) for o in outs]
    return (xp.reshape(batch, seq, d), xs.reshape(db, dseq, d), *stk)
```

```python
import functools
from typing import NamedTuple

import jax
import jax.numpy as jnp
from jax import lax
from jax.experimental import pallas as pl
from jax.experimental.pallas import tpu as pltpu

F32 = jnp.float32
BF16 = jnp.bfloat16

LANES = 128
SUBLANES = 8
VMEM_LIMIT = 56 * 1024 * 1024
RMS_EPS = 1e-6
LOG2E = 1.4426950408889634
HGRN_CHUNK = 128
ROW_TILE = 512
PAGES_PER_STEP = 8
FOX_HEADS_PER_STEP = 1
HGRN_PAGES_PER_STEP = 16
BIAS_PAGE_TILE = 256
HEAD_PAD = 16


def _cparams(*sem):
    return pltpu.CompilerParams(dimension_semantics=sem, vmem_limit_bytes=VMEM_LIMIT)


def _dot(a, b):
    return jnp.dot(a, b, preferred_element_type=F32)


def _dot_nt(a, b):
    return lax.dot_general(a, b, (((1,), (1,)), ((), ())), preferred_element_type=F32)


def _dot_tn(a, b):
    return lax.dot_general(a, b, (((0,), (0,)), ((), ())), preferred_element_type=F32)


def _split3(x):
    hi = x.astype(BF16)
    r = x - hi.astype(F32)
    mid = r.astype(BF16)
    lo = (r - mid.astype(F32)).astype(BF16)
    return hi, mid, lo


def _dot01(m01, x):
    hi, mid, lo = _split3(x)
    return _dot(m01, hi) + _dot(m01, mid) + _dot(m01, lo)


def _dot01_nt(x, m01):
    hi, mid, lo = _split3(x)
    return _dot(hi, m01) + _dot(mid, m01) + _dot(lo, m01)


def _sigmoid(x):
    return 1.0 / (1.0 + jnp.exp(-x))


def _silu(x):
    return x * _sigmoid(x)


def _rms(x, w):
    ms = jnp.mean(x * x, axis=-1, keepdims=True)
    return x * lax.rsqrt(ms + RMS_EPS) * w


def _largest_tile(n, cap, align):
    for t in range(min(cap, n), 0, -1):
        if n % t == 0 and t % align == 0:
            return t
    return n


def _ada_kernel(c_ref, w_ref, b_ref, o_ref):
    s = _silu(c_ref[...])
    s_hi = s.astype(BF16)
    s_lo = (s - s_hi.astype(F32)).astype(BF16)
    w = w_ref[...]
    w_hi = w.astype(BF16)
    w_lo = (w - w_hi.astype(F32)).astype(BF16)
    o_ref[...] = _dot(s_hi, w_hi) + _dot(s_hi, w_lo) + _dot(s_lo, w_hi) + b_ref[...]


def _ada_call(c_all, w_ada, b_ada):
    r, d = c_all.shape
    n = w_ada.shape[1]
    tn = 512
    return pl.pallas_call(
        _ada_kernel,
        grid=(n // tn,),
        in_specs=[pl.BlockSpec((r, d), lambda j: (0, 0)),
                  pl.BlockSpec((d, tn), lambda j: (0, j)),
                  pl.BlockSpec((1, tn), lambda j: (0, j))],
        out_specs=pl.BlockSpec((r, tn), lambda j: (0, j)),
        out_shape=jax.ShapeDtypeStruct((r, n), F32),
        compiler_params=_cparams("parallel"),
        name="ada_mod",
    )(c_all, w_ada, b_ada.reshape(1, n))


def _mod_spec(mod3, d, col, tiles_per_group):
    return pl.BlockSpec((None, mod3.shape[1], d), lambda i: (i // tiles_per_group, 0, col))


def _proj_first_kernel(x_ref, nw_ref, shift_ref, scale_ref, w_ref, wf_ref, bf_ref,
                       h_ref, hq_ref, logf_ref, c_ref, carry_ref, wbf_ref, *, heads, tiles_per_seq):
    _weights_once(w_ref, wbf_ref)
    xn = _rms(x_ref[...], nw_ref[...])
    h = (xn * (1.0 + scale_ref[...]) + shift_ref[...]).astype(BF16)
    h_ref[...] = h
    hq_ref[...] = _dot(h, wbf_ref[...]).astype(BF16)
    x = _dot(h, wf_ref[...].astype(BF16)) + bf_ref[...]
    lf = jnp.minimum(x, 0.0) - jnp.log1p(jnp.exp(-jnp.abs(x)))
    logf_ref[...] = lf[:, :heads]
    if tiles_per_seq is None:
        c_ref[...] = lf[:, :heads]
        return
    tm = lf.shape[0]

    @pl.when(pl.program_id(0) % tiles_per_seq == 0)
    def _():
        carry_ref[...] = jnp.zeros_like(carry_ref)

    row = lax.broadcasted_iota(jnp.int32, (LANES, LANES), 0)
    col = lax.broadcasted_iota(jnp.int32, (LANES, LANES), 1)
    tril = jnp.where(col <= row, 1.0, 0.0).astype(BF16)
    carry = carry_ref[...]
    for i in range(tm // LANES):
        c = _dot01(tril, lf[i * LANES:(i + 1) * LANES]) + carry
        c_ref[i * LANES:(i + 1) * LANES, :] = c[:, :heads]
        carry = c[LANES - 1:LANES, :]
    carry_ref[...] = carry


def _proj_first_call(x2, norm_w, mod3, tiles_per_group, w_all, seg_w, w_ff, b_ff, heads, tm, seq_len):
    m, d = x2.shape
    row = lambda width: pl.BlockSpec((tm, width), lambda i: (i, 0))
    tiles_per_seq = None if seq_len == 1 else seq_len // tm
    return pl.pallas_call(
        functools.partial(_proj_first_kernel, heads=heads, tiles_per_seq=tiles_per_seq),
        grid=(m // tm,),
        in_specs=[row(d), pl.BlockSpec((1, d), lambda i: (0, 0)),
                  _mod_spec(mod3, d, 0, tiles_per_group), _mod_spec(mod3, d, 1, tiles_per_group),
                  pl.BlockSpec((d, seg_w), lambda i: (0, 0)),
                  pl.BlockSpec(w_ff.shape, lambda i: (0, 0)), pl.BlockSpec(b_ff.shape, lambda i: (0, 0))],
        out_specs=(row(d), row(seg_w), row(heads), row(heads)),
        out_shape=(jax.ShapeDtypeStruct((m, d), BF16), jax.ShapeDtypeStruct((m, seg_w), BF16),
                   jax.ShapeDtypeStruct((m, heads), F32), jax.ShapeDtypeStruct((m, heads), F32)),
        scratch_shapes=[pltpu.VMEM((1, LANES), F32), pltpu.VMEM((d, seg_w), BF16)],
        compiler_params=_cparams("arbitrary"),
        name="proj_first",
    )(x2, norm_w.reshape(1, d), mod3, mod3, w_all, w_ff, b_ff)


def _weights_once(w_ref, wbf_ref):
    @pl.when(pl.program_id(0) == 0)
    def _():
        wbf_ref[...] = w_ref[...].astype(BF16)


def _project(h_ref, w_ref, wbf_ref):
    _weights_once(w_ref, wbf_ref)
    return _dot(h_ref[...], wbf_ref[...])


def _proj_plain_kernel(h_ref, w_ref, o_ref, wbf_ref, *, act):
    z = _project(h_ref, w_ref, wbf_ref)
    if act:
        z = _silu(z)
    o_ref[...] = z.astype(o_ref.dtype)


def _proj_v_kernel(h_ref, w_ref, o_ref, wbf_ref):
    o_ref[...] = _project(h_ref, w_ref, wbf_ref)


def _proj_vt_kernel(h_ref, w_ref, o_ref, ot_ref, wbf_ref):
    z = _project(h_ref, w_ref, wbf_ref)
    o_ref[...] = z
    ot_ref[...] = z.T.astype(BF16)


def _proj_forget_kernel(h_ref, w_ref, lbl_ref, logf_ref, k_ref, wbf_ref, *, layer):
    z = _project(h_ref, w_ref, wbf_ref)
    lbl = lbl_ref[...]
    e = jnp.exp(lbl - jnp.max(lbl, axis=0, keepdims=True))
    lb = jnp.sum(e[:layer + 1], axis=0, keepdims=True) / jnp.sum(e, axis=0, keepdims=True)
    t = jnp.exp(-jnp.abs(z))
    big = 1.0 / (1.0 + t)
    small = t * big
    pos = z >= 0.0
    logf_ref[...] = jnp.log(lb + (1.0 - lb) * jnp.where(pos, big, small))
    k_ref[...] = ((1.0 - lb) * jnp.where(pos, small, big)).astype(BF16)


def _aug_lanes(c2, is_query, tm):
    hi, mid, lo = _split3(c2)
    hi, mid, lo = hi.astype(F32), mid.astype(F32), lo.astype(F32)
    lane = lax.broadcasted_iota(jnp.int32, (tm, LANES), 1)
    one = jnp.ones((tm, LANES), F32)
    zero = jnp.zeros((tm, LANES), F32)
    if is_query:
        parts = jnp.where(lane == 0, hi, jnp.where(lane == 1, mid, jnp.where(lane == 2, lo, zero)))
        return jnp.where((lane >= 3) & (lane < 6), one, parts).astype(BF16)
    parts = jnp.where(lane == 3, -hi, jnp.where(lane == 4, -mid, jnp.where(lane == 5, -lo, zero)))
    return jnp.where(lane < 3, one, parts).astype(BF16)


def _proj_qk_kernel(h_ref, w_ref, nw_ref, c_ref, *refs, heads, hd, is_query, qscale):
    out_refs, wbf_ref = refs[:-1], refs[-1]
    z = _project(h_ref, w_ref, wbf_ref)
    tm = z.shape[0]
    nw = nw_ref[...]
    c = c_ref[...]
    for h in range(heads):
        n = _rms(z[:, h * hd:(h + 1) * hd], nw)
        aug = _aug_lanes(c[:, h:h + 1] * LOG2E, is_query, tm)
        if is_query:
            (aug_ref,) = out_refs
            n = n * qscale
        else:
            full_ref, aug_ref = out_refs
            full_ref[:, h * hd:(h + 1) * hd] = n
        aug_ref[:, 2 * h * hd:(2 * h + 1) * hd] = n.astype(BF16)
        aug_ref[:, (2 * h + 1) * hd:(2 * h + 2) * hd] = aug


def _proj_call(kernel, h, w, seg_w, seg, extra_inputs, extra_specs, out_shapes, out_specs, tm,
               name="in_proj"):
    m, d = h.shape
    return pl.pallas_call(
        kernel,
        grid=(m // tm,),
        in_specs=[pl.BlockSpec((tm, d), lambda i: (i, 0)),
                  pl.BlockSpec((d, seg_w), lambda i: (0, seg))] + list(extra_specs),
        out_specs=out_specs,
        out_shape=out_shapes,
        scratch_shapes=[pltpu.VMEM((d, seg_w), BF16)],
        compiler_params=_cparams("arbitrary"),
        name=name,
    )(h, w, *extra_inputs)


def _full_spec(shape):
    nd = len(shape)
    return pl.BlockSpec(shape, lambda i: (0,) * nd)


def _in_proj(x2, norm_w, mod3, tiles_per_group, w_all, w_ff, b_ff, lb_logits, q_norm_w, k_norm_w,
             layer, tm, seq_len, hgrn_heads, dk, fox_heads, hd, with_vt=False):
    m, d = x2.shape
    hk = hgrn_heads * dk
    dfox = fox_heads * hd
    row = lambda width: pl.BlockSpec((tm, width), lambda i: (i, 0))
    sds = lambda width, dt: jax.ShapeDtypeStruct((m, width), dt)
    assert hk == dfox, "segments are indexed as equal-width column blocks"
    h, hq, f_logf, f_c = _proj_first_call(x2, norm_w, mod3, tiles_per_group, w_all, hk, w_ff, b_ff,
                                          fox_heads, tm, seq_len)
    plain = lambda seg, act, nm: _proj_call(
        functools.partial(_proj_plain_kernel, act=act), h, w_all, hk, seg, (), (),
        sds(hk, BF16), row(hk), tm, name=nm)
    a_logf, a_k = _proj_call(
        functools.partial(_proj_forget_kernel, layer=layer), h, w_all, hk, 1,
        (lb_logits,), (_full_spec(lb_logits.shape),),
        (sds(hk, F32), sds(hk, BF16)), (row(hk), row(hk)), tm, name="proj_hf")
    hi = plain(2, False, "proj_hi")
    sg_h = plain(3, True, "proj_hg")
    qk = lambda seg, nw, is_query, outs, specs, nm: _proj_call(
        functools.partial(_proj_qk_kernel, heads=fox_heads, hd=hd, is_query=is_query,
                          qscale=hd ** -0.5 * LOG2E),
        h, w_all, dfox, seg, (nw.reshape(1, hd), f_c), (_full_spec((1, hd)), row(fox_heads)),
        outs, specs, tm, name=nm)
    q_aug = qk(4, q_norm_w, True, sds(2 * dfox, BF16), row(2 * dfox), "proj_fq")
    f_k, k_aug = qk(5, k_norm_w, False, (sds(dfox, F32), sds(2 * dfox, BF16)),
                    (row(dfox), row(2 * dfox)), "proj_fk")
    if not with_vt:
        f_v = _proj_call(_proj_v_kernel, h, w_all, dfox, 6, (), (), sds(dfox, F32), row(dfox), tm,
                         name="proj_fv")
        v_t = None
    else:
        f_v, v_t = _proj_call(
            _proj_vt_kernel, h, w_all, dfox, 6, (), (),
            (sds(dfox, F32), jax.ShapeDtypeStruct((m // tm, dfox, tm), BF16)),
            (row(dfox), pl.BlockSpec((None, dfox, tm), lambda i: (i, 0, 0))), tm, name="proj_fv")
    sg_f = plain(7, True, "proj_fg")
    return dict(hq=hq, a_logf=a_logf, a_k=a_k, hi=hi, sg_h=sg_h, f_logf=f_logf, f_c=f_c,
                q_aug=q_aug, f_k=f_k, k_aug=k_aug, f_v=f_v, v_t=v_t, sg_f=sg_f)


def _hgrn_levels(c):
    levels = []
    m = c // 2
    while m >= 1:
        levels.append(m)
        m //= 2
    return levels


def _hgrn_level_exponents(b, g, c):
    dk = b.shape[1]
    ridx = lax.broadcasted_iota(jnp.int32, (c, dk), 0)
    out = []
    for m in _hgrn_levels(c):
        pos = ridx & (2 * m - 1)
        later = pos >= m
        if 2 * m >= SUBLANES:
            b3 = b.reshape(c // (2 * m), 2 * m, dk)
            bm = jnp.broadcast_to(b3[:, m - 1:m, :], b3.shape).reshape(c, dk)
            e = jnp.where(later, b - bm, bm - b)
        elif m == 2:
            g_next = pltpu.roll(g, c - 1, 0)
            g_prev = pltpu.roll(g, 1, 0)
            e = jnp.where(pos == 0, g_next,
                          jnp.where(pos == 2, g, jnp.where(pos == 3, g + g_prev, 0.0)))
        else:
            e = jnp.where(later, g, 0.0)
        out.append((e, later))
    return out


def _hgrn_kernel(pt_ref, q_ref, k_ref, g_ref, v_ref, sg_ref, nw_ref, *rest, c, heads, dk, dv, dec):
    ci = pl.program_id(1)
    if dec is None:
        o_ref, s_ref, st_ref = rest
    else:
        n_in = _dec_num_inputs(dec)
        (o_ref, s_ref, do_ref), st_ref = rest[n_in:n_in + 3], rest[n_in + 3]
        lin = pl.program_id(0) * pl.num_programs(1) + ci
        dec_args = (dec, pt_ref, lin, rest[:n_in], do_ref, rest[n_in + 4:])

    @pl.when(ci == 0)
    def _():
        st_ref[...] = jnp.zeros_like(st_ref)

    dec_state = None if dec is None else _dec_main(*dec_args)

    row = lax.broadcasted_iota(jnp.int32, (c, c), 0)
    col = lax.broadcasted_iota(jnp.int32, (c, c), 1)
    tril = jnp.where(col <= row, 1.0, 0.0).astype(BF16)
    eye = row == col
    masks = []
    for m in _hgrn_levels(c):
        same = (row // (2 * m)) == (col // (2 * m))
        masks.append(same & ((row & (2 * m - 1)) >= m) & ((col & (2 * m - 1)) < m))
    nw = nw_ref[...]

    for h in range(heads):
        ks = slice(h * dk, (h + 1) * dk)
        vs = slice(h * dv, (h + 1) * dv)
        q = q_ref[:, ks].astype(F32)
        k = k_ref[:, ks].astype(F32)
        g = g_ref[:, ks] * LOG2E
        v = v_ref[:, vs]
        b = _dot01(tril, g)
        b_last = b[c - 1:c]
        a = jnp.where(eye, _dot_nt(q.astype(BF16), k.astype(BF16)), 0.0)
        for (e, later), mask in zip(_hgrn_level_exponents(b, g, c), masks):
            x = (jnp.where(later, q, k) * jnp.exp2(e)).astype(BF16)
            a = jnp.where(mask, _dot_nt(x, x), a)
        st = st_ref[h]
        o = _dot_nt((q * jnp.exp2(b)).astype(BF16), st.astype(BF16)) + _dot(a.astype(BF16), v)
        kd = (k * jnp.exp2(b_last - b)).astype(BF16)
        st_ref[h] = st * jnp.exp2(b_last) + _dot_tn(v, kd)
        o_ref[:, vs] = (_rms(o, nw) * sg_ref[:, vs].astype(F32)).astype(BF16)

    @pl.when(ci == pl.num_programs(1) - 1)
    def _():
        for h in range(heads):
            s_ref[h] = st_ref[h].T

    if dec is not None:
        _dec_finish(*dec_args, dec_state)


def _hgrn_steps(batch, seq):
    return batch * (seq // HGRN_CHUNK)


def _hgrn_prompt_call(p, hgrn_norm_w, batch, seq, heads, dk, dv, page_table, dec=None, dec_ops=()):
    c = HGRN_CHUNK
    nc = seq // c
    blk = lambda width: pl.BlockSpec((c, heads * width), lambda b, i, pt: (b * nc + i, 0))
    in_specs = [blk(dk), blk(dk), blk(dk), blk(dv), blk(dv),
                pl.BlockSpec((1, dv), lambda b, i, pt: (0, 0))]
    out_specs = [blk(dv), pl.BlockSpec((None, heads, dk, dv), lambda b, i, pt: (b, 0, 0, 0))]
    out_shape = [jax.ShapeDtypeStruct((batch * seq, heads * dv), BF16),
                 jax.ShapeDtypeStruct((batch, heads, dk, dv), F32)]
    scratch = [pltpu.VMEM((heads, dv, dk), F32)]
    if dec is not None:
        d_in, d_out, d_shape, d_scratch = _dec_specs(dec, lambda b, i: b * nc + i)
        in_specs += d_in
        out_specs.append(d_out)
        out_shape.append(d_shape)
        scratch += d_scratch
    return pl.pallas_call(
        functools.partial(_hgrn_kernel, c=c, heads=heads, dk=dk, dv=dv, dec=dec),
        grid_spec=pltpu.PrefetchScalarGridSpec(
            num_scalar_prefetch=1, grid=(batch, nc), in_specs=in_specs, out_specs=out_specs,
            scratch_shapes=scratch),
        out_shape=out_shape,
        compiler_params=_cparams("parallel" if dec is None else "arbitrary", "arbitrary"),
        name="hgrn_prompt",
    )(page_table, p["hq"], p["a_k"], p["a_logf"], p["hi"], p["sg_h"], hgrn_norm_w.reshape(1, dv),
      *dec_ops)


def _hgrn_step_kernel(q_ref, k_ref, g_ref, v_ref, sg_ref, nw_ref, s0_ref, o_ref, s_ref, *, heads):
    pad = lambda x: jnp.concatenate([x, jnp.zeros((LANES - heads, x.shape[1]), F32)], axis=0)
    qt = pad(q_ref[...].astype(F32)).T
    kt = pad(k_ref[...].astype(F32)).T
    ft = pad(jnp.exp(g_ref[...])).T
    v = v_ref[...].astype(F32)
    sg = sg_ref[...].astype(F32)
    rows = []
    for h in range(heads):
        s = ft[:, h:h + 1] * s0_ref[h] + kt[:, h:h + 1] * v[h:h + 1, :]
        s_ref[h] = s
        o = jnp.sum(qt[:, h:h + 1] * s, axis=0, keepdims=True)
        rows.append(_rms(o, nw_ref[...]) * sg[h:h + 1, :])
    o_ref[...] = jnp.concatenate(rows, axis=0).astype(BF16)


def _hgrn_step_call(p, hgrn_norm_w, state, heads, dk, dv):
    db = state.shape[0]
    r3 = lambda x, w: x.reshape(db, heads, w)
    blk = lambda w: pl.BlockSpec((None, heads, w), lambda b: (b, 0, 0))
    st_blk = pl.BlockSpec((None, heads, dk, dv), lambda b: (b, 0, 0, 0))
    o, s = pl.pallas_call(
        functools.partial(_hgrn_step_kernel, heads=heads),
        grid=(db,),
        in_specs=[blk(dk), blk(dk), blk(dk), blk(dv), blk(dv),
                  pl.BlockSpec((1, dv), lambda b: (0, 0)), st_blk],
        out_specs=(blk(dv), st_blk),
        out_shape=(jax.ShapeDtypeStruct((db, heads, dv), BF16),
                   jax.ShapeDtypeStruct(state.shape, F32)),
        compiler_params=_cparams("parallel"),
        name="hgrn_step",
    )(r3(p["hq"], dk), r3(p["a_k"], dk), r3(p["a_logf"], dk), r3(p["hi"], dv), r3(p["sg_h"], dv),
      hgrn_norm_w.reshape(1, dv), state)
    return o.reshape(db, heads * dv), s


def _fox_kernel(pt_ref, q_ref, k_ref, vt_ref, sg_ref, *rest, t, hpb, dec):
    qi = pl.program_id(2)
    if dec is None:
        o_ref, s_ref, m_ref, l_ref, acc_ref = rest
        dec_state = None
    else:
        n_in = _dec_num_inputs(dec)
        o_ref, do_ref, s_ref, m_ref, l_ref, acc_ref = rest[n_in:n_in + 6]
        lin = ((pl.program_id(0) * pl.num_programs(1) + pl.program_id(1)) * pl.num_programs(2) + qi)
        dec_args = (dec, pt_ref, lin, rest[:n_in], do_ref, rest[n_in + 6:])
        dec_state = _dec_main(*dec_args)
    hd = acc_ref.shape[1]
    units = range(hpb)
    m_ref[...] = jnp.full_like(m_ref, -jnp.inf)
    l_ref[...] = jnp.zeros_like(l_ref)
    acc_ref[...] = jnp.zeros_like(acc_ref)

    def scores(j, u):
        cols = slice(2 * u * hd, 2 * (u + 1) * hd)
        return _dot_nt(k_ref[pl.ds(pl.multiple_of(j * t, t), t), cols], q_ref[:, cols])

    def update(j, u, s):
        m_prev = m_ref[u]
        m_new = jnp.maximum(m_prev, jnp.max(s, axis=0, keepdims=True))
        alpha = jnp.exp2(m_prev - m_new)
        p = jnp.exp2(s - m_new)
        l_ref[u] = alpha * l_ref[u] + jnp.sum(p, axis=0, keepdims=True)
        acc_ref[u] = alpha * acc_ref[u] + _dot(vt_ref[j, u * hd:(u + 1) * hd, :], p.astype(BF16))
        m_ref[u] = m_new

    for u in units:
        s_ref[u] = scores(0, u)

    def body(j, carry):
        s_cur = [s_ref[u] for u in units]
        s_next = [scores(j + 1, u) for u in units]
        for u in units:
            update(j, u, s_cur[u])
        for u in units:
            s_ref[u] = s_next[u]
        return carry

    lax.fori_loop(0, qi, body, 0)
    key = lax.broadcasted_iota(jnp.int32, (t, t), 0)
    qry = lax.broadcasted_iota(jnp.int32, (t, t), 1)
    for u in units:
        update(qi, u, jnp.where(key <= qry, s_ref[u], -jnp.inf))
    for u in units:
        o = (acc_ref[u] / l_ref[u]).T
        o_ref[:, u * hd:(u + 1) * hd] = (o * sg_ref[:, u * hd:(u + 1) * hd].astype(F32)).astype(BF16)
    if dec is not None:
        _dec_finish(*dec_args, dec_state)


def _fox_heads_per_step(heads):
    return _largest_tile(heads, FOX_HEADS_PER_STEP, 1)


def _fox_steps(batch, seq, heads, t):
    return batch * (heads // _fox_heads_per_step(heads)) * (seq // t)


def _fox_prompt_call(p, batch, seq, heads, hd, t, page_table, dec=None, dec_ops=()):
    nq = seq // t
    hpb = _fox_heads_per_step(heads)
    hg = heads // hpb
    in_specs = [pl.BlockSpec((t, 2 * hd * hpb), lambda b, h, i, pt: (b * nq + i, h)),
                pl.BlockSpec((seq, 2 * hd * hpb), lambda b, h, i, pt: (b, h)),
                pl.BlockSpec((nq, hd * hpb, t), lambda b, h, i, pt: (b, h, 0)),
                pl.BlockSpec((t, hd * hpb), lambda b, h, i, pt: (b * nq + i, h))]
    out_specs = [pl.BlockSpec((t, hd * hpb), lambda b, h, i, pt: (b * nq + i, h))]
    out_shape = [jax.ShapeDtypeStruct((batch * seq, heads * hd), BF16)]
    scratch = [pltpu.VMEM((hpb, t, t), F32), pltpu.VMEM((hpb, 1, t), F32),
               pltpu.VMEM((hpb, 1, t), F32), pltpu.VMEM((hpb, hd, t), F32)]
    if dec is not None:
        d_in, d_out, d_shape, d_scratch = _dec_specs(dec, lambda b, h, i: (b * hg + h) * nq + i)
        in_specs += d_in
        out_specs.append(d_out)
        out_shape.append(d_shape)
        scratch += d_scratch
    return pl.pallas_call(
        functools.partial(_fox_kernel, t=t, hpb=hpb, dec=dec),
        grid_spec=pltpu.PrefetchScalarGridSpec(
            num_scalar_prefetch=1, grid=(batch, hg, nq), in_specs=in_specs, out_specs=out_specs,
            scratch_shapes=scratch),
        out_shape=out_shape,
        compiler_params=_cparams(*(["parallel" if dec is None else "arbitrary"] * 2), "arbitrary"),
        name="fox_prompt",
    )(page_table, p["q_aug"], p["k_aug"], p["v_t"], p["sg_f"], *dec_ops)


def _page_bias_kernel(x_ref, u_ref, o_ref, *, heads):
    x = x_ref[...]
    n = x.shape[1]
    o_ref[:, :n] = _dot01_nt(x, u_ref[...])
    t = x[:, :LANES]
    for i in range(1, n // LANES):
        t = t + x[:, i * LANES:(i + 1) * LANES]
    shift = heads
    while shift < LANES:
        t = t + pltpu.roll(t, shift, 1)
        shift *= 2
    o_ref[:, n:] = jnp.concatenate([t] * (n // LANES), axis=1)


def _page_bias_call(lf_flat, heads):
    n_pages, n = lf_flat.shape
    assert LANES % heads == 0 and n % LANES == 0
    idx = jnp.arange(n, dtype=jnp.int32)
    later = (idx[:, None] // heads) > (idx[None, :] // heads)
    same_head = (idx[:, None] % heads) == (idx[None, :] % heads)
    u = (later & same_head).astype(BF16)
    pb = _largest_tile(n_pages, BIAS_PAGE_TILE, SUBLANES)
    return pl.pallas_call(
        functools.partial(_page_bias_kernel, heads=heads),
        grid=(n_pages // pb,),
        in_specs=[pl.BlockSpec((pb, n), lambda i: (i, 0)), pl.BlockSpec((n, n), lambda i: (0, 0))],
        out_specs=pl.BlockSpec((pb, 2 * n), lambda i: (i, 0)),
        out_shape=jax.ShapeDtypeStruct((n_pages, 2 * n), F32),
        compiler_params=_cparams("parallel"),
        name="page_bias",
    )(lf_flat, u)


class _Dec(NamedTuple):
    layer: int
    seq0: int
    nseq: int
    n_pg: int
    pps: int
    rows: int
    heads: int
    hd: int
    steps: int

    @property
    def groups(self):
        return self.n_pg // self.pps

    @property
    def guarded(self):
        return self.steps != self.nseq * self.groups


def _dec_where(dec, lin):
    active = lin < dec.nseq * dec.groups
    local = jnp.minimum(lin // dec.groups, dec.nseq - 1)
    grp = jnp.where(active, lin % dec.groups, dec.groups - 1)
    return active, local, grp


def _dec_num_inputs(dec):
    return 2 + 3 * dec.pps


_SEQ_Q, _SEQ_K, _SEQ_V, _SEQ_GATE = range(4)


def _dec_specs(dec, lin_of):
    heads, hd = dec.heads, dec.hd
    n = dec.rows * heads

    def seq_blk(shape, base):
        def imap(*a):
            return (base + _dec_where(dec, lin_of(*a[:-1]))[1], 0, 0)
        return pl.BlockSpec((None,) + shape, imap)

    def page(a, slot):
        _, local, grp = _dec_where(dec, lin_of(*a[:-1]))
        return a[-1][dec.seq0 + local, dec.n_pg - 1 - (grp * dec.pps + slot)]

    def cache_blk(slot):
        return pl.BlockSpec((None, None, dec.rows, heads, hd),
                            lambda *a: (dec.layer, page(a, slot), 0, 0, 0))

    def bias_blk(slot):
        return pl.BlockSpec((SUBLANES, 2 * n), lambda *a: (page(a, slot) // SUBLANES, 0))

    slots = range(dec.pps)
    in_specs = ([seq_blk((4 * HEAD_PAD, hd), dec.seq0), seq_blk((1, n), dec.seq0)]
                + [cache_blk(s) for s in slots] + [cache_blk(s) for s in slots]
                + [bias_blk(s) for s in slots])
    scratch = [pltpu.VMEM((HEAD_PAD, LANES), F32), pltpu.VMEM((HEAD_PAD, LANES), F32),
               pltpu.VMEM((1, n), F32), pltpu.VMEM((HEAD_PAD, hd), F32)]
    out_shape = jax.ShapeDtypeStruct((dec.nseq, HEAD_PAD, hd), BF16)
    return in_specs, seq_blk((HEAD_PAD, hd), 0), out_shape, scratch


def _seq_rows(seq_ref, which):
    return seq_ref[which * HEAD_PAD:(which + 1) * HEAD_PAD, :]


def _dec_pages(dec, pt_ref, local, grp, in_refs, scratch):
    pps, heads, hd = dec.pps, dec.heads, dec.hd
    seq_ref, cn_ref = in_refs[:2]
    k_refs = in_refs[2:2 + pps]
    v_refs = in_refs[2 + pps:2 + 2 * pps]
    b_refs = in_refs[2 + 2 * pps:2 + 3 * pps]
    m_ref, l_ref, run_ref, acc_ref = scratch
    n = dec.rows * heads

    @pl.when(grp == 0)
    def _():
        m_ref[...] = jnp.full_like(m_ref, -jnp.inf)
        l_ref[...] = jnp.zeros_like(l_ref)
        run_ref[...] = jnp.zeros_like(run_ref)
        acc_ref[...] = jnp.zeros_like(acc_ref)

    q = _seq_rows(seq_ref, _SEQ_Q).astype(BF16)
    sub = lax.broadcasted_iota(jnp.int32, (HEAD_PAD, n), 0)
    lane = lax.broadcasted_iota(jnp.int32, (HEAD_PAD, n), 1)
    own = (lane % heads) == (sub % heads)
    cn = cn_ref[...]
    run = run_ref[...]
    scores = []
    for p in range(pps):
        r = pt_ref[dec.seq0 + local, dec.n_pg - 1 - (grp * pps + p)] % SUBLANES
        kb = k_refs[p][...].reshape(n, hd).astype(BF16)
        bias = (b_refs[p][pl.ds(r, 1), :n] + run + cn) * LOG2E
        scores.append(jnp.where(own, _dot_nt(q, kb) + bias, -jnp.inf))
        run = run + b_refs[p][pl.ds(r, 1), n:]
    run_ref[...] = run

    m_prev = m_ref[:, :1]
    m_new = m_prev
    for s in scores:
        m_new = jnp.maximum(m_new, jnp.max(s, axis=-1, keepdims=True))
    alpha = jnp.exp2(m_prev - m_new)
    l_new = alpha * l_ref[:, :1]
    acc = alpha * acc_ref[...]
    for p in range(pps):
        pr = jnp.exp2(scores[p] - m_new)
        l_new = l_new + jnp.sum(pr, axis=-1, keepdims=True)
        acc = acc + _dot(pr.astype(BF16), v_refs[p][...].reshape(n, hd).astype(BF16))
    m_ref[...] = jnp.broadcast_to(m_new, m_ref.shape)
    l_ref[...] = jnp.broadcast_to(l_new, l_ref.shape)
    acc_ref[...] = acc
    return m_new, l_new, acc


def _dec_new_row(dec, grp, in_refs, o_ref, state):
    seq_ref = in_refs[0]
    m_new, l_new, acc = state

    @pl.when(grp == dec.groups - 1)
    def _():
        s_new = jnp.sum(_seq_rows(seq_ref, _SEQ_Q) * _seq_rows(seq_ref, _SEQ_K), axis=-1,
                        keepdims=True)
        m_fin = jnp.maximum(m_new, s_new)
        a_fin = jnp.exp2(m_new - m_fin)
        p_new = jnp.exp2(s_new - m_fin)
        l_fin = a_fin * l_new + p_new
        out = (a_fin * acc + p_new * _seq_rows(seq_ref, _SEQ_V)) / l_fin
        o_ref[...] = (out * _seq_rows(seq_ref, _SEQ_GATE)).astype(BF16)


def _dec_main(dec, pt_ref, lin, in_refs, o_ref, scratch):
    active, local, grp = _dec_where(dec, lin)
    if not dec.guarded:
        return _dec_pages(dec, pt_ref, local, grp, in_refs, scratch)

    @pl.when(active)
    def _():
        state = _dec_pages(dec, pt_ref, local, grp, in_refs, scratch)
        _dec_new_row(dec, grp, in_refs, o_ref, state)
    return None


def _dec_finish(dec, pt_ref, lin, in_refs, o_ref, scratch, state):
    del pt_ref, scratch
    if state is not None:
        _dec_new_row(dec, _dec_where(dec, lin)[2], in_refs, o_ref, state)


def _decode_kernel(pt_ref, *refs, dec):
    n_in = _dec_num_inputs(dec)
    args = (dec, pt_ref, pl.program_id(0), refs[:n_in], refs[n_in], refs[n_in + 1:])
    _dec_finish(*args, _dec_main(*args))


def _decode_operands(p, cache_k, cache_v, page_bias, pps, heads, hd):
    db = p["f_logf"].shape[0]
    rows = cache_k.shape[2]
    padh = lambda x: jnp.pad(x.astype(F32), ((0, 0), (0, HEAD_PAD - heads), (0, 0)))
    head_rows = lambda x: padh(x.reshape(db, heads, hd))
    main = lambda x: padh(x.reshape(db, heads, 2 * hd)[:, :, :hd])
    seq = jnp.concatenate([main(p["q_aug"]), main(p["k_aug"]), head_rows(p["f_v"]),
                           head_rows(p["sg_f"])], axis=1)
    cn = jnp.tile(p["f_logf"], (1, rows)).reshape(db, 1, rows * heads)
    return [seq, cn] + [cache_k] * pps + [cache_v] * pps + [page_bias] * pps


def _decode_call(dec, dec_ops, page_table):
    in_specs, out_spec, out_shape, scratch = _dec_specs(dec, lambda i: i)
    return pl.pallas_call(
        functools.partial(_decode_kernel, dec=dec),
        grid_spec=pltpu.PrefetchScalarGridSpec(
            num_scalar_prefetch=1, grid=(dec.steps,), in_specs=in_specs, out_specs=out_spec,
            scratch_shapes=scratch),
        out_shape=out_shape,
        compiler_params=_cparams("arbitrary"),
        name="fox_decode",
    )(page_table, *dec_ops)


def _out_kernel(a_ref, b_ref, wa_ref, wb_ref, x_ref, gate_ref, y_ref):
    out = _dot(a_ref[...], wa_ref[...]) + _dot(b_ref[...], wb_ref[...])
    y_ref[...] = x_ref[...] + gate_ref[...] * out


def _out_call(a, b, w_out_bf, x2, mod3, tm, tiles_per_group):
    m, d = x2.shape
    da, db_ = a.shape[1], b.shape[1]
    assert da == db_, "w_out is read as two equal row blocks"
    return pl.pallas_call(
        _out_kernel,
        grid=(m // tm,),
        in_specs=[pl.BlockSpec((tm, da), lambda i: (i, 0)),
                  pl.BlockSpec((tm, db_), lambda i: (i, 0)),
                  pl.BlockSpec((da, d), lambda i: (0, 0)),
                  pl.BlockSpec((db_, d), lambda i: (1, 0)),
                  pl.BlockSpec((tm, d), lambda i: (i, 0)),
                  _mod_spec(mod3, d, 2, tiles_per_group)],
        out_specs=pl.BlockSpec((tm, d), lambda i: (i, 0)),
        out_shape=jax.ShapeDtypeStruct((m, d), F32),
        compiler_params=_cparams("parallel"),
        name="out_proj",
    )(a, b, w_out_bf, w_out_bf, x2, mod3)


def kernel(x_prompt, x_sample, c_prompt, c_sample, cache_k, cache_v, cache_logf, state_hgrn,
           page_table, norm_w, w_ada, b_ada, w_in, b_fox_f, lb_logits, q_norm_w, k_norm_w,
           hgrn_norm_w, w_out):
    batch, seq, d = x_prompt.shape
    db, dseq, _ = x_sample.shape
    assert dseq == 1, "the sample group is one new row per sequence"
    depth, n_phys, page_rows, fheads, hd = cache_k.shape
    _, _, hheads, dk, dv = state_hgrn.shape
    hk, dh, dfox = hheads * dk, hheads * dv, fheads * hd
    n_seg = 2 * hk + 2 * dh + 4 * dfox
    assert w_in.shape[2] == n_seg + fheads
    mp, ms = batch * seq, db * dseq
    tm = _largest_tile(seq, ROW_TILE, LANES)

    xp = x_prompt.reshape(mp, d)
    xs = x_sample.reshape(ms, d)
    r_pad = -(batch + db) % SUBLANES
    c_all = jnp.concatenate([c_prompt, c_sample, jnp.zeros((r_pad, d), F32)], axis=0)

    outs = [[] for _ in range(8)]
    for l in range(depth):
        mod = _ada_call(c_all, w_ada[l], b_ada[l])
        mod_p = mod[:batch].reshape(batch, 1, 3 * d)
        mod_s = mod[batch:batch + db].reshape(1, db, 3 * d)
        w_all = w_in[l]
        w_ff = jnp.pad(w_all[:, n_seg:], ((0, 0), (0, LANES - fheads)))
        b_ff = jnp.pad(b_fox_f[l].astype(F32), (0, LANES - fheads)).reshape(1, LANES)
        w_out_bf = w_out[l].astype(BF16)
        proj = functools.partial(_in_proj, norm_w=norm_w[l], w_all=w_all, w_ff=w_ff, b_ff=b_ff,
                                 lb_logits=lb_logits, q_norm_w=q_norm_w[l], k_norm_w=k_norm_w[l],
                                 layer=l, hgrn_heads=hheads, dk=dk, fox_heads=fheads, hd=hd)

        pp = proj(xp, mod3=mod_p, tiles_per_group=seq // tm, tm=tm, seq_len=seq, with_vt=True)
        ps = proj(xs, mod3=mod_s, tiles_per_group=1, tm=ms, seq_len=1)

        page_bias = _page_bias_call(cache_logf[l].reshape(n_phys, page_rows * fheads), fheads)
        n_pg = page_table.shape[1]
        pps = _largest_tile(n_pg, PAGES_PER_STEP, 1)
        pps_h = _largest_tile(n_pg, HGRN_PAGES_PER_STEP, 1)
        fox_steps, hgrn_steps = _fox_steps(batch, seq, fheads, tm), _hgrn_steps(batch, seq)
        n_fox = min(db - db // 2, fox_steps // (n_pg // pps))
        n_hgrn = min(db - n_fox, hgrn_steps // (n_pg // pps_h))
        n_rest = db - n_fox - n_hgrn
        plan = lambda seq0, nseq, steps, pages: None if nseq == 0 else _Dec(
            layer=l, seq0=seq0, nseq=nseq, n_pg=n_pg, pps=pages, rows=page_rows, heads=fheads,
            hd=hd, steps=steps)
        dec_fox = plan(0, n_fox, fox_steps, pps)
        dec_hgrn = plan(n_fox, n_hgrn, hgrn_steps, pps_h)
        dec_rest = plan(n_fox + n_hgrn, n_rest, n_rest * (n_pg // pps), pps)
        hosted = lambda dec: () if dec is None else tuple(
            _decode_operands(ps, cache_k, cache_v, page_bias, dec.pps, fheads, hd))

        a_p, s_p, *o_hgrn = _hgrn_prompt_call(pp, hgrn_norm_w[l], batch, seq, hheads, dk, dv,
                                              page_table, dec_hgrn, hosted(dec_hgrn))
        b_p, *o_fox = _fox_prompt_call(pp, batch, seq, fheads, hd, tm, page_table, dec_fox,
                                       hosted(dec_fox))
        xp = _out_call(a_p, b_p, w_out_bf, xp, mod_p, tm, seq // tm)
        outs[0].append(pp["f_k"].reshape(batch, seq, fheads, hd))
        outs[1].append(pp["f_v"].reshape(batch, seq, fheads, hd))
        outs[2].append(pp["f_logf"].reshape(batch, seq, fheads))
        outs[3].append(s_p)

        a_s, s_s = _hgrn_step_call(ps, hgrn_norm_w[l], state_hgrn[l], hheads, dk, dv)
        o_rest = [] if dec_rest is None else [_decode_call(dec_rest, hosted(dec_rest), page_table)]
        b_s = jnp.concatenate(o_fox + o_hgrn + o_rest, axis=0)[:, :fheads, :].reshape(db, dfox)
        xs = _out_call(a_s, b_s, w_out_bf, xs, mod_s, ms, 1)
        outs[4].append(ps["f_k"].reshape(db, dseq, fheads, hd))
        outs[5].append(ps["f_v"].reshape(db, dseq, fheads, hd))
        outs[6].append(ps["f_logf"].reshape(db, dseq, fheads))
        outs[7].append(s_s)

    stk = [jnp.stack(o) for o in outs]
    return (xp.reshape(batch, seq, d), xs.reshape(db, dseq, d), *stk)
```

```python
import functools
from typing import NamedTuple

import jax
import jax.numpy as jnp
from jax import lax
from jax.experimental import pallas as pl
from jax.experimental.pallas import tpu as pltpu

F32 = jnp.float32
BF16 = jnp.bfloat16

LANES = 128
SUBLANES = 8
VMEM_LIMIT = 56 * 1024 * 1024
RMS_EPS = 1e-6
LOG2E = 1.4426950408889634
HGRN_CHUNK = 128
ROW_TILE = 512
PAGES_PER_STEP = 8
FOX_HEADS_PER_STEP = 1
HGRN_PAGES_PER_STEP = 16
BIAS_PAGE_TILE = 256
HEAD_PAD = 16


def _cparams(*sem):
    return pltpu.CompilerParams(dimension_semantics=sem, vmem_limit_bytes=VMEM_LIMIT)


def _dot(a, b):
    return jnp.dot(a, b, preferred_element_type=F32)


def _dot_nt(a, b):
    return lax.dot_general(a, b, (((1,), (1,)), ((), ())), preferred_element_type=F32)


def _dot_tn(a, b):
    return lax.dot_general(a, b, (((0,), (0,)), ((), ())), preferred_element_type=F32)


def _split3(x):
    hi = x.astype(BF16)
    r = x - hi.astype(F32)
    mid = r.astype(BF16)
    lo = (r - mid.astype(F32)).astype(BF16)
    return hi, mid, lo


def _dot01(m01, x):
    hi, mid, lo = _split3(x)
    return _dot(m01, hi) + _dot(m01, mid) + _dot(m01, lo)


def _dot01_nt(x, m01):
    hi, mid, lo = _split3(x)
    return _dot(hi, m01) + _dot(mid, m01) + _dot(lo, m01)


def _sigmoid(x):
    return 1.0 / (1.0 + jnp.exp(-x))


def _silu(x):
    return x * _sigmoid(x)


def _rms(x, w):
    ms = jnp.mean(x * x, axis=-1, keepdims=True)
    return x * lax.rsqrt(ms + RMS_EPS) * w


def _largest_tile(n, cap, align):
    for t in range(min(cap, n), 0, -1):
        if n % t == 0 and t % align == 0:
            return t
    return n


def _ada_kernel(c_ref, w_ref, b_ref, o_ref):
    s = _silu(c_ref[...])
    s_hi = s.astype(BF16)
    s_lo = (s - s_hi.astype(F32)).astype(BF16)
    w = w_ref[...]
    w_hi = w.astype(BF16)
    w_lo = (w - w_hi.astype(F32)).astype(BF16)
    o_ref[...] = _dot(s_hi, w_hi) + _dot(s_hi, w_lo) + _dot(s_lo, w_hi) + b_ref[...]


def _ada_call(c_all, w_ada, b_ada):
    r, d = c_all.shape
    n = w_ada.shape[1]
    tn = 512
    return pl.pallas_call(
        _ada_kernel,
        grid=(n // tn,),
        in_specs=[pl.BlockSpec((r, d), lambda j: (0, 0)),
                  pl.BlockSpec((d, tn), lambda j: (0, j)),
                  pl.BlockSpec((1, tn), lambda j: (0, j))],
        out_specs=pl.BlockSpec((r, tn), lambda j: (0, j)),
        out_shape=jax.ShapeDtypeStruct((r, n), F32),
        compiler_params=_cparams("parallel"),
        name="ada_mod",
    )(c_all, w_ada, b_ada.reshape(1, n))


def _mod_spec(mod3, d, col, tiles_per_group):
    return pl.BlockSpec((None, mod3.shape[1], d), lambda i: (i // tiles_per_group, 0, col))


def _proj_first_kernel(x_ref, nw_ref, shift_ref, scale_ref, w_ref, wf_ref, bf_ref,
                       h_ref, hq_ref, logf_ref, c_ref, carry_ref, wbf_ref, *, heads, tiles_per_seq):
    _weights_once(w_ref, wbf_ref)
    xn = _rms(x_ref[...], nw_ref[...])
    h = (xn * (1.0 + scale_ref[...]) + shift_ref[...]).astype(BF16)
    h_ref[...] = h
    hq_ref[...] = _dot_nt(h, wbf_ref[...]).astype(BF16)
    x = _dot_nt(h, wf_ref[...].astype(BF16)) + bf_ref[...]
    lf = jnp.minimum(x, 0.0) - jnp.log1p(jnp.exp(-jnp.abs(x)))
    logf_ref[...] = lf[:, :heads]
    if tiles_per_seq is None:
        c_ref[...] = lf[:, :heads]
        return
    tm = lf.shape[0]

    @pl.when(pl.program_id(0) % tiles_per_seq == 0)
    def _():
        carry_ref[...] = jnp.zeros_like(carry_ref)

    row = lax.broadcasted_iota(jnp.int32, (LANES, LANES), 0)
    col = lax.broadcasted_iota(jnp.int32, (LANES, LANES), 1)
    tril = jnp.where(col <= row, 1.0, 0.0).astype(BF16)
    carry = carry_ref[...]
    for i in range(tm // LANES):
        c = _dot01(tril, lf[i * LANES:(i + 1) * LANES]) + carry
        c_ref[i * LANES:(i + 1) * LANES, :] = c[:, :heads]
        carry = c[LANES - 1:LANES, :]
    carry_ref[...] = carry


def _proj_first_call(x2, norm_w, mod3, tiles_per_group, w_all, seg_w, w_ff, b_ff, heads, tm, seq_len):
    m, d = x2.shape
    row = lambda width: pl.BlockSpec((tm, width), lambda i: (i, 0))
    tiles_per_seq = None if seq_len == 1 else seq_len // tm
    return pl.pallas_call(
        functools.partial(_proj_first_kernel, heads=heads, tiles_per_seq=tiles_per_seq),
        grid=(m // tm,),
        in_specs=[row(d), pl.BlockSpec((1, d), lambda i: (0, 0)),
                  _mod_spec(mod3, d, 0, tiles_per_group), _mod_spec(mod3, d, 1, tiles_per_group),
                  pl.BlockSpec((seg_w, d), lambda i: (0, 0)),
                  pl.BlockSpec(w_ff.shape, lambda i: (0, 0)), pl.BlockSpec(b_ff.shape, lambda i: (0, 0))],
        out_specs=(row(d), row(seg_w), row(heads), row(heads)),
        out_shape=(jax.ShapeDtypeStruct((m, d), BF16), jax.ShapeDtypeStruct((m, seg_w), BF16),
                   jax.ShapeDtypeStruct((m, heads), F32), jax.ShapeDtypeStruct((m, heads), F32)),
        scratch_shapes=[pltpu.VMEM((1, LANES), F32), pltpu.VMEM((seg_w, d), BF16)],
        compiler_params=_cparams("arbitrary"),
        name="proj_first",
    )(x2, norm_w.reshape(1, d), mod3, mod3, w_all, w_ff, b_ff)


def _weights_once(w_ref, wbf_ref):
    @pl.when(pl.program_id(0) == 0)
    def _():
        wbf_ref[...] = w_ref[...].astype(BF16)


def _project(h_ref, w_ref, wbf_ref):
    _weights_once(w_ref, wbf_ref)
    return _dot_nt(h_ref[...], wbf_ref[...])


def _proj_plain_kernel(h_ref, w_ref, o_ref, wbf_ref, *, act):
    z = _project(h_ref, w_ref, wbf_ref)
    if act:
        z = _silu(z)
    o_ref[...] = z.astype(o_ref.dtype)


def _proj_v_kernel(h_ref, w_ref, o_ref, wbf_ref):
    o_ref[...] = _project(h_ref, w_ref, wbf_ref)


def _proj_vt_kernel(h_ref, w_ref, o_ref, ot_ref, wbf_ref):
    z = _project(h_ref, w_ref, wbf_ref)
    o_ref[...] = z
    ot_ref[...] = z.T.astype(BF16)


def _proj_forget_kernel(h_ref, w_ref, lbl_ref, logf_ref, k_ref, wbf_ref, *, layer):
    z = _project(h_ref, w_ref, wbf_ref)
    lbl = lbl_ref[...]
    e = jnp.exp(lbl - jnp.max(lbl, axis=0, keepdims=True))
    lb = jnp.sum(e[:layer + 1], axis=0, keepdims=True) / jnp.sum(e, axis=0, keepdims=True)
    t = jnp.exp(-jnp.abs(z))
    big = 1.0 / (1.0 + t)
    small = t * big
    pos = z >= 0.0
    logf_ref[...] = jnp.log(lb + (1.0 - lb) * jnp.where(pos, big, small))
    k_ref[...] = ((1.0 - lb) * jnp.where(pos, small, big)).astype(BF16)


def _aug_lanes(c2, is_query, tm):
    hi, mid, lo = _split3(c2)
    hi, mid, lo = hi.astype(F32), mid.astype(F32), lo.astype(F32)
    lane = lax.broadcasted_iota(jnp.int32, (tm, LANES), 1)
    one = jnp.ones((tm, LANES), F32)
    zero = jnp.zeros((tm, LANES), F32)
    if is_query:
        parts = jnp.where(lane == 0, hi, jnp.where(lane == 1, mid, jnp.where(lane == 2, lo, zero)))
        return jnp.where((lane >= 3) & (lane < 6), one, parts).astype(BF16)
    parts = jnp.where(lane == 3, -hi, jnp.where(lane == 4, -mid, jnp.where(lane == 5, -lo, zero)))
    return jnp.where(lane < 3, one, parts).astype(BF16)


def _proj_qk_kernel(h_ref, w_ref, nw_ref, c_ref, *refs, heads, hd, is_query, qscale):
    out_refs, wbf_ref = refs[:-1], refs[-1]
    z = _project(h_ref, w_ref, wbf_ref)
    tm = z.shape[0]
    nw = nw_ref[...]
    c = c_ref[...]
    for h in range(heads):
        n = _rms(z[:, h * hd:(h + 1) * hd], nw)
        aug = _aug_lanes(c[:, h:h + 1] * LOG2E, is_query, tm)
        if is_query:
            (aug_ref,) = out_refs
            n = n * qscale
        else:
            full_ref, aug_ref = out_refs
            full_ref[:, h * hd:(h + 1) * hd] = n
        aug_ref[:, 2 * h * hd:(2 * h + 1) * hd] = n.astype(BF16)
        aug_ref[:, (2 * h + 1) * hd:(2 * h + 2) * hd] = aug


def _proj_call(kernel, h, w, seg_w, seg, extra_inputs, extra_specs, out_shapes, out_specs, tm,
               name="in_proj"):
    m, d = h.shape
    return pl.pallas_call(
        kernel,
        grid=(m // tm,),
        in_specs=[pl.BlockSpec((tm, d), lambda i: (i, 0)),
                  pl.BlockSpec((seg_w, d), lambda i: (seg, 0))] + list(extra_specs),
        out_specs=out_specs,
        out_shape=out_shapes,
        scratch_shapes=[pltpu.VMEM((seg_w, d), BF16)],
        compiler_params=_cparams("arbitrary"),
        name=name,
    )(h, w, *extra_inputs)


def _full_spec(shape):
    nd = len(shape)
    return pl.BlockSpec(shape, lambda i: (0,) * nd)


def _in_proj(x2, norm_w, mod3, tiles_per_group, w_all, w_ff, b_ff, lb_logits, q_norm_w, k_norm_w,
             layer, tm, seq_len, hgrn_heads, dk, fox_heads, hd, with_vt=False):
    m, d = x2.shape
    hk = hgrn_heads * dk
    dfox = fox_heads * hd
    row = lambda width: pl.BlockSpec((tm, width), lambda i: (i, 0))
    sds = lambda width, dt: jax.ShapeDtypeStruct((m, width), dt)
    assert hk == dfox, "segments are indexed as equal-width column blocks"
    h, hq, f_logf, f_c = _proj_first_call(x2, norm_w, mod3, tiles_per_group, w_all, hk, w_ff, b_ff,
                                          fox_heads, tm, seq_len)
    plain = lambda seg, act, nm: _proj_call(
        functools.partial(_proj_plain_kernel, act=act), h, w_all, hk, seg, (), (),
        sds(hk, BF16), row(hk), tm, name=nm)
    a_logf, a_k = _proj_call(
        functools.partial(_proj_forget_kernel, layer=layer), h, w_all, hk, 1,
        (lb_logits,), (_full_spec(lb_logits.shape),),
        (sds(hk, F32), sds(hk, BF16)), (row(hk), row(hk)), tm, name="proj_hf")
    hi = plain(2, False, "proj_hi")
    sg_h = plain(3, True, "proj_hg")
    qk = lambda seg, nw, is_query, outs, specs, nm: _proj_call(
        functools.partial(_proj_qk_kernel, heads=fox_heads, hd=hd, is_query=is_query,
                          qscale=hd ** -0.5 * LOG2E),
        h, w_all, dfox, seg, (nw.reshape(1, hd), f_c), (_full_spec((1, hd)), row(fox_heads)),
        outs, specs, tm, name=nm)
    q_aug = qk(4, q_norm_w, True, sds(2 * dfox, BF16), row(2 * dfox), "proj_fq")
    f_k, k_aug = qk(5, k_norm_w, False, (sds(dfox, F32), sds(2 * dfox, BF16)),
                    (row(dfox), row(2 * dfox)), "proj_fk")
    if not with_vt:
        f_v = _proj_call(_proj_v_kernel, h, w_all, dfox, 6, (), (), sds(dfox, F32), row(dfox), tm,
                         name="proj_fv")
        v_t = None
    else:
        f_v, v_t = _proj_call(
            _proj_vt_kernel, h, w_all, dfox, 6, (), (),
            (sds(dfox, F32), jax.ShapeDtypeStruct((m // tm, dfox, tm), BF16)),
            (row(dfox), pl.BlockSpec((None, dfox, tm), lambda i: (i, 0, 0))), tm, name="proj_fv")
    sg_f = plain(7, True, "proj_fg")
    return dict(hq=hq, a_logf=a_logf, a_k=a_k, hi=hi, sg_h=sg_h, f_logf=f_logf, f_c=f_c,
                q_aug=q_aug, f_k=f_k, k_aug=k_aug, f_v=f_v, v_t=v_t, sg_f=sg_f)


def _hgrn_levels(c):
    levels = []
    m = c // 2
    while m >= 1:
        levels.append(m)
        m //= 2
    return levels


def _hgrn_level_exponents(b, g, c):
    dk = b.shape[1]
    ridx = lax.broadcasted_iota(jnp.int32, (c, dk), 0)
    out = []
    for m in _hgrn_levels(c):
        pos = ridx & (2 * m - 1)
        later = pos >= m
        if 2 * m >= SUBLANES:
            b3 = b.reshape(c // (2 * m), 2 * m, dk)
            bm = jnp.broadcast_to(b3[:, m - 1:m, :], b3.shape).reshape(c, dk)
            e = jnp.where(later, b - bm, bm - b)
        elif m == 2:
            g_next = pltpu.roll(g, c - 1, 0)
            g_prev = pltpu.roll(g, 1, 0)
            e = jnp.where(pos == 0, g_next,
                          jnp.where(pos == 2, g, jnp.where(pos == 3, g + g_prev, 0.0)))
        else:
            e = jnp.where(later, g, 0.0)
        out.append((e, later))
    return out


def _hgrn_kernel(pt_ref, q_ref, k_ref, g_ref, v_ref, sg_ref, nw_ref, *rest, c, heads, dk, dv, dec):
    ci = pl.program_id(1)
    if dec is None:
        o_ref, s_ref, st_ref = rest
    else:
        n_in = _dec_num_inputs(dec)
        (o_ref, s_ref, do_ref), st_ref = rest[n_in:n_in + 3], rest[n_in + 3]
        lin = pl.program_id(0) * pl.num_programs(1) + ci
        dec_args = (dec, pt_ref, lin, rest[:n_in], do_ref, rest[n_in + 4:])

    @pl.when(ci == 0)
    def _():
        st_ref[...] = jnp.zeros_like(st_ref)

    dec_state = None if dec is None else _dec_main(*dec_args)

    row = lax.broadcasted_iota(jnp.int32, (c, c), 0)
    col = lax.broadcasted_iota(jnp.int32, (c, c), 1)
    tril = jnp.where(col <= row, 1.0, 0.0).astype(BF16)
    eye = row == col
    masks = []
    for m in _hgrn_levels(c):
        same = (row // (2 * m)) == (col // (2 * m))
        masks.append(same & ((row & (2 * m - 1)) >= m) & ((col & (2 * m - 1)) < m))
    nw = nw_ref[...]

    for h in range(heads):
        ks = slice(h * dk, (h + 1) * dk)
        vs = slice(h * dv, (h + 1) * dv)
        q = q_ref[:, ks].astype(F32)
        k = k_ref[:, ks].astype(F32)
        g = g_ref[:, ks] * LOG2E
        v = v_ref[:, vs]
        b = _dot01(tril, g)
        b_last = b[c - 1:c]
        a = jnp.where(eye, _dot_nt(q.astype(BF16), k.astype(BF16)), 0.0)
        for (e, later), mask in zip(_hgrn_level_exponents(b, g, c), masks):
            x = (jnp.where(later, q, k) * jnp.exp2(e)).astype(BF16)
            a = jnp.where(mask, _dot_nt(x, x), a)
        st = st_ref[h]
        o = _dot_nt((q * jnp.exp2(b)).astype(BF16), st.astype(BF16)) + _dot(a.astype(BF16), v)
        kd = (k * jnp.exp2(b_last - b)).astype(BF16)
        st_ref[h] = st * jnp.exp2(b_last) + _dot_tn(v, kd)
        o_ref[:, vs] = (_rms(o, nw) * sg_ref[:, vs].astype(F32)).astype(BF16)

    @pl.when(ci == pl.num_programs(1) - 1)
    def _():
        for h in range(heads):
            s_ref[h] = st_ref[h].T

    if dec is not None:
        _dec_finish(*dec_args, dec_state)


def _hgrn_steps(batch, seq):
    return batch * (seq // HGRN_CHUNK)


def _hgrn_prompt_call(p, hgrn_norm_w, batch, seq, heads, dk, dv, page_table, dec=None, dec_ops=()):
    c = HGRN_CHUNK
    nc = seq // c
    blk = lambda width: pl.BlockSpec((c, heads * width), lambda b, i, pt: (b * nc + i, 0))
    in_specs = [blk(dk), blk(dk), blk(dk), blk(dv), blk(dv),
                pl.BlockSpec((1, dv), lambda b, i, pt: (0, 0))]
    out_specs = [blk(dv), pl.BlockSpec((None, heads, dk, dv), lambda b, i, pt: (b, 0, 0, 0))]
    out_shape = [jax.ShapeDtypeStruct((batch * seq, heads * dv), BF16),
                 jax.ShapeDtypeStruct((batch, heads, dk, dv), F32)]
    scratch = [pltpu.VMEM((heads, dv, dk), F32)]
    if dec is not None:
        d_in, d_out, d_shape, d_scratch = _dec_specs(dec, lambda b, i: b * nc + i)
        in_specs += d_in
        out_specs.append(d_out)
        out_shape.append(d_shape)
        scratch += d_scratch
    return pl.pallas_call(
        functools.partial(_hgrn_kernel, c=c, heads=heads, dk=dk, dv=dv, dec=dec),
        grid_spec=pltpu.PrefetchScalarGridSpec(
            num_scalar_prefetch=1, grid=(batch, nc), in_specs=in_specs, out_specs=out_specs,
            scratch_shapes=scratch),
        out_shape=out_shape,
        compiler_params=_cparams("parallel" if dec is None else "arbitrary", "arbitrary"),
        name="hgrn_prompt",
    )(page_table, p["hq"], p["a_k"], p["a_logf"], p["hi"], p["sg_h"], hgrn_norm_w.reshape(1, dv),
      *dec_ops)


def _hgrn_step_kernel(q_ref, k_ref, g_ref, v_ref, sg_ref, nw_ref, s0_ref, o_ref, s_ref, *, heads):
    pad = lambda x: jnp.concatenate([x, jnp.zeros((LANES - heads, x.shape[1]), F32)], axis=0)
    qt = pad(q_ref[...].astype(F32)).T
    kt = pad(k_ref[...].astype(F32)).T
    ft = pad(jnp.exp(g_ref[...])).T
    v = v_ref[...].astype(F32)
    sg = sg_ref[...].astype(F32)
    rows = []
    for h in range(heads):
        s = ft[:, h:h + 1] * s0_ref[h] + kt[:, h:h + 1] * v[h:h + 1, :]
        s_ref[h] = s
        o = jnp.sum(qt[:, h:h + 1] * s, axis=0, keepdims=True)
        rows.append(_rms(o, nw_ref[...]) * sg[h:h + 1, :])
    o_ref[...] = jnp.concatenate(rows, axis=0).astype(BF16)


def _hgrn_step_call(p, hgrn_norm_w, state, heads, dk, dv):
    db = state.shape[0]
    r3 = lambda x, w: x.reshape(db, heads, w)
    blk = lambda w: pl.BlockSpec((None, heads, w), lambda b: (b, 0, 0))
    st_blk = pl.BlockSpec((None, heads, dk, dv), lambda b: (b, 0, 0, 0))
    o, s = pl.pallas_call(
        functools.partial(_hgrn_step_kernel, heads=heads),
        grid=(db,),
        in_specs=[blk(dk), blk(dk), blk(dk), blk(dv), blk(dv),
                  pl.BlockSpec((1, dv), lambda b: (0, 0)), st_blk],
        out_specs=(blk(dv), st_blk),
        out_shape=(jax.ShapeDtypeStruct((db, heads, dv), BF16),
                   jax.ShapeDtypeStruct(state.shape, F32)),
        compiler_params=_cparams("parallel"),
        name="hgrn_step",
    )(r3(p["hq"], dk), r3(p["a_k"], dk), r3(p["a_logf"], dk), r3(p["hi"], dv), r3(p["sg_h"], dv),
      hgrn_norm_w.reshape(1, dv), state)
    return o.reshape(db, heads * dv), s


def _fox_kernel(pt_ref, q_ref, k_ref, vt_ref, sg_ref, *rest, t, hpb, dec):
    qi = pl.program_id(2)
    if dec is None:
        o_ref, s_ref, m_ref, l_ref, acc_ref = rest
        dec_state = None
    else:
        n_in = _dec_num_inputs(dec)
        o_ref, do_ref, s_ref, m_ref, l_ref, acc_ref = rest[n_in:n_in + 6]
        lin = ((pl.program_id(0) * pl.num_programs(1) + pl.program_id(1)) * pl.num_programs(2) + qi)
        dec_args = (dec, pt_ref, lin, rest[:n_in], do_ref, rest[n_in + 6:])
        dec_state = _dec_main(*dec_args)
    hd = acc_ref.shape[1]
    units = range(hpb)
    m_ref[...] = jnp.full_like(m_ref, -jnp.inf)
    l_ref[...] = jnp.zeros_like(l_ref)
    acc_ref[...] = jnp.zeros_like(acc_ref)

    cols = [slice(2 * u * hd, 2 * (u + 1) * hd) for u in units]
    qs = [q_ref[:, c] for c in cols]

    def scores(j, u):
        return _dot_nt(k_ref[pl.ds(pl.multiple_of(j * t, t), t), cols[u]], qs[u])

    def update(j, u, s):
        m_prev = m_ref[u]
        m_new = jnp.maximum(m_prev, jnp.max(s, axis=0, keepdims=True))
        alpha = jnp.exp2(m_prev - m_new)
        p = jnp.exp2(s - m_new)
        l_ref[u] = alpha * l_ref[u] + jnp.sum(p, axis=0, keepdims=True)
        acc_ref[u] = alpha * acc_ref[u] + _dot(vt_ref[j, u * hd:(u + 1) * hd, :], p.astype(BF16))
        m_ref[u] = m_new

    for u in units:
        s_ref[u] = scores(0, u)

    def body(j, carry):
        s_cur = [s_ref[u] for u in units]
        s_next = [scores(j + 1, u) for u in units]
        for u in units:
            update(j, u, s_cur[u])
        for u in units:
            s_ref[u] = s_next[u]
        return carry

    lax.fori_loop(0, qi, body, 0)
    key = lax.broadcasted_iota(jnp.int32, (t, t), 0)
    qry = lax.broadcasted_iota(jnp.int32, (t, t), 1)
    for u in units:
        update(qi, u, jnp.where(key <= qry, s_ref[u], -jnp.inf))
    for u in units:
        o = (acc_ref[u] / l_ref[u]).T
        o_ref[:, u * hd:(u + 1) * hd] = (o * sg_ref[:, u * hd:(u + 1) * hd].astype(F32)).astype(BF16)
    if dec is not None:
        _dec_finish(*dec_args, dec_state)


def _fox_heads_per_step(heads):
    return _largest_tile(heads, FOX_HEADS_PER_STEP, 1)


def _fox_steps(batch, seq, heads, t):
    return batch * (heads // _fox_heads_per_step(heads)) * (seq // t)


def _fox_prompt_call(p, batch, seq, heads, hd, t, page_table, dec=None, dec_ops=()):
    nq = seq // t
    hpb = _fox_heads_per_step(heads)
    hg = heads // hpb
    in_specs = [pl.BlockSpec((t, 2 * hd * hpb), lambda b, h, i, pt: (b * nq + i, h)),
                pl.BlockSpec((seq, 2 * hd * hpb), lambda b, h, i, pt: (b, h)),
                pl.BlockSpec((nq, hd * hpb, t), lambda b, h, i, pt: (b, h, 0)),
                pl.BlockSpec((t, hd * hpb), lambda b, h, i, pt: (b * nq + i, h))]
    out_specs = [pl.BlockSpec((t, hd * hpb), lambda b, h, i, pt: (b * nq + i, h))]
    out_shape = [jax.ShapeDtypeStruct((batch * seq, heads * hd), BF16)]
    scratch = [pltpu.VMEM((hpb, t, t), F32), pltpu.VMEM((hpb, 1, t), F32),
               pltpu.VMEM((hpb, 1, t), F32), pltpu.VMEM((hpb, hd, t), F32)]
    if dec is not None:
        d_in, d_out, d_shape, d_scratch = _dec_specs(dec, lambda b, h, i: (b * hg + h) * nq + i)
        in_specs += d_in
        out_specs.append(d_out)
        out_shape.append(d_shape)
        scratch += d_scratch
    return pl.pallas_call(
        functools.partial(_fox_kernel, t=t, hpb=hpb, dec=dec),
        grid_spec=pltpu.PrefetchScalarGridSpec(
            num_scalar_prefetch=1, grid=(batch, hg, nq), in_specs=in_specs, out_specs=out_specs,
            scratch_shapes=scratch),
        out_shape=out_shape,
        compiler_params=_cparams(*(["parallel" if dec is None else "arbitrary"] * 2), "arbitrary"),
        name="fox_prompt",
    )(page_table, p["q_aug"], p["k_aug"], p["v_t"], p["sg_f"], *dec_ops)


def _page_bias_kernel(x_ref, u_ref, o_ref, *, heads):
    x = x_ref[...]
    n = x.shape[1]
    o_ref[:, :n] = _dot01_nt(x, u_ref[...])
    t = x[:, :LANES]
    for i in range(1, n // LANES):
        t = t + x[:, i * LANES:(i + 1) * LANES]
    shift = heads
    while shift < LANES:
        t = t + pltpu.roll(t, shift, 1)
        shift *= 2
    o_ref[:, n:] = jnp.concatenate([t] * (n // LANES), axis=1)


def _page_bias_call(lf_flat, heads):
    n_pages, n = lf_flat.shape
    assert LANES % heads == 0 and n % LANES == 0
    idx = jnp.arange(n, dtype=jnp.int32)
    later = (idx[:, None] // heads) > (idx[None, :] // heads)
    same_head = (idx[:, None] % heads) == (idx[None, :] % heads)
    u = (later & same_head).astype(BF16)
    pb = _largest_tile(n_pages, BIAS_PAGE_TILE, SUBLANES)
    return pl.pallas_call(
        functools.partial(_page_bias_kernel, heads=heads),
        grid=(n_pages // pb,),
        in_specs=[pl.BlockSpec((pb, n), lambda i: (i, 0)), pl.BlockSpec((n, n), lambda i: (0, 0))],
        out_specs=pl.BlockSpec((pb, 2 * n), lambda i: (i, 0)),
        out_shape=jax.ShapeDtypeStruct((n_pages, 2 * n), F32),
        compiler_params=_cparams("parallel"),
        name="page_bias",
    )(lf_flat, u)


class _Dec(NamedTuple):
    layer: int
    seq0: int
    nseq: int
    n_pg: int
    pps: int
    rows: int
    heads: int
    hd: int
    steps: int

    @property
    def groups(self):
        return self.n_pg // self.pps

    @property
    def guarded(self):
        return self.steps != self.nseq * self.groups


def _dec_where(dec, lin):
    active = lin < dec.nseq * dec.groups
    local = jnp.minimum(lin // dec.groups, dec.nseq - 1)
    grp = jnp.where(active, lin % dec.groups, dec.groups - 1)
    return active, local, grp


def _dec_num_inputs(dec):
    return 2 + 3 * dec.pps


_SEQ_Q, _SEQ_K, _SEQ_V, _SEQ_GATE = range(4)


def _dec_specs(dec, lin_of):
    heads, hd = dec.heads, dec.hd
    n = dec.rows * heads

    def seq_blk(shape, base):
        def imap(*a):
            return (base + _dec_where(dec, lin_of(*a[:-1]))[1], 0, 0)
        return pl.BlockSpec((None,) + shape, imap)

    def page(a, slot):
        _, local, grp = _dec_where(dec, lin_of(*a[:-1]))
        return a[-1][dec.seq0 + local, dec.n_pg - 1 - (grp * dec.pps + slot)]

    def cache_blk(slot):
        return pl.BlockSpec((None, None, dec.rows, heads, hd),
                            lambda *a: (dec.layer, page(a, slot), 0, 0, 0))

    def bias_blk(slot):
        return pl.BlockSpec((SUBLANES, 2 * n), lambda *a: (page(a, slot) // SUBLANES, 0))

    slots = range(dec.pps)
    in_specs = ([seq_blk((4 * HEAD_PAD, hd), dec.seq0), seq_blk((1, n), dec.seq0)]
                + [cache_blk(s) for s in slots] + [cache_blk(s) for s in slots]
                + [bias_blk(s) for s in slots])
    scratch = [pltpu.VMEM((HEAD_PAD, LANES), F32), pltpu.VMEM((HEAD_PAD, LANES), F32),
               pltpu.VMEM((1, n), F32), pltpu.VMEM((HEAD_PAD, hd), F32)]
    out_shape = jax.ShapeDtypeStruct((dec.nseq, HEAD_PAD, hd), BF16)
    return in_specs, seq_blk((HEAD_PAD, hd), 0), out_shape, scratch


def _seq_rows(seq_ref, which):
    return seq_ref[which * HEAD_PAD:(which + 1) * HEAD_PAD, :]


def _dec_pages(dec, pt_ref, local, grp, in_refs, scratch):
    pps, heads, hd = dec.pps, dec.heads, dec.hd
    seq_ref, cn_ref = in_refs[:2]
    k_refs = in_refs[2:2 + pps]
    v_refs = in_refs[2 + pps:2 + 2 * pps]
    b_refs = in_refs[2 + 2 * pps:2 + 3 * pps]
    m_ref, l_ref, run_ref, acc_ref = scratch
    n = dec.rows * heads

    @pl.when(grp == 0)
    def _():
        m_ref[...] = jnp.full_like(m_ref, -jnp.inf)
        l_ref[...] = jnp.zeros_like(l_ref)
        run_ref[...] = jnp.zeros_like(run_ref)
        acc_ref[...] = jnp.zeros_like(acc_ref)

    q = _seq_rows(seq_ref, _SEQ_Q).astype(BF16)
    sub = lax.broadcasted_iota(jnp.int32, (HEAD_PAD, n), 0)
    lane = lax.broadcasted_iota(jnp.int32, (HEAD_PAD, n), 1)
    own = (lane % heads) == (sub % heads)
    cn = cn_ref[...]
    run = run_ref[...]
    scores = []
    for p in range(pps):
        r = pt_ref[dec.seq0 + local, dec.n_pg - 1 - (grp * pps + p)] % SUBLANES
        kb = k_refs[p][...].reshape(n, hd).astype(BF16)
        bias = (b_refs[p][pl.ds(r, 1), :n] + run + cn) * LOG2E
        scores.append(jnp.where(own, _dot_nt(q, kb) + bias, -jnp.inf))
        run = run + b_refs[p][pl.ds(r, 1), n:]
    run_ref[...] = run

    m_prev = m_ref[:, :1]
    m_new = m_prev
    for s in scores:
        m_new = jnp.maximum(m_new, jnp.max(s, axis=-1, keepdims=True))
    alpha = jnp.exp2(m_prev - m_new)
    l_new = alpha * l_ref[:, :1]
    acc = alpha * acc_ref[...]
    for p in range(pps):
        pr = jnp.exp2(scores[p] - m_new)
        l_new = l_new + jnp.sum(pr, axis=-1, keepdims=True)
        acc = acc + _dot(pr.astype(BF16), v_refs[p][...].reshape(n, hd).astype(BF16))
    m_ref[...] = jnp.broadcast_to(m_new, m_ref.shape)
    l_ref[...] = jnp.broadcast_to(l_new, l_ref.shape)
    acc_ref[...] = acc
    return m_new, l_new, acc


def _dec_new_row(dec, grp, in_refs, o_ref, state):
    seq_ref = in_refs[0]
    m_new, l_new, acc = state

    @pl.when(grp == dec.groups - 1)
    def _():
        s_new = jnp.sum(_seq_rows(seq_ref, _SEQ_Q) * _seq_rows(seq_ref, _SEQ_K), axis=-1,
                        keepdims=True)
        m_fin = jnp.maximum(m_new, s_new)
        a_fin = jnp.exp2(m_new - m_fin)
        p_new = jnp.exp2(s_new - m_fin)
        l_fin = a_fin * l_new + p_new
        out = (a_fin * acc + p_new * _seq_rows(seq_ref, _SEQ_V)) / l_fin
        o_ref[...] = (out * _seq_rows(seq_ref, _SEQ_GATE)).astype(BF16)


def _dec_main(dec, pt_ref, lin, in_refs, o_ref, scratch):
    active, local, grp = _dec_where(dec, lin)
    if not dec.guarded:
        return _dec_pages(dec, pt_ref, local, grp, in_refs, scratch)

    @pl.when(active)
    def _():
        state = _dec_pages(dec, pt_ref, local, grp, in_refs, scratch)
        _dec_new_row(dec, grp, in_refs, o_ref, state)
    return None


def _dec_finish(dec, pt_ref, lin, in_refs, o_ref, scratch, state):
    del pt_ref, scratch
    if state is not None:
        _dec_new_row(dec, _dec_where(dec, lin)[2], in_refs, o_ref, state)


def _decode_kernel(pt_ref, *refs, dec):
    n_in = _dec_num_inputs(dec)
    args = (dec, pt_ref, pl.program_id(0), refs[:n_in], refs[n_in], refs[n_in + 1:])
    _dec_finish(*args, _dec_main(*args))


def _decode_operands(p, cache_k, cache_v, page_bias, pps, heads, hd):
    db = p["f_logf"].shape[0]
    rows = cache_k.shape[2]
    padh = lambda x: jnp.pad(x.astype(F32), ((0, 0), (0, HEAD_PAD - heads), (0, 0)))
    head_rows = lambda x: padh(x.reshape(db, heads, hd))
    main = lambda x: padh(x.reshape(db, heads, 2 * hd)[:, :, :hd])
    seq = jnp.concatenate([main(p["q_aug"]), main(p["k_aug"]), head_rows(p["f_v"]),
                           head_rows(p["sg_f"])], axis=1)
    cn = jnp.tile(p["f_logf"], (1, rows)).reshape(db, 1, rows * heads)
    return [seq, cn] + [cache_k] * pps + [cache_v] * pps + [page_bias] * pps


def _decode_call(dec, dec_ops, page_table):
    in_specs, out_spec, out_shape, scratch = _dec_specs(dec, lambda i: i)
    return pl.pallas_call(
        functools.partial(_decode_kernel, dec=dec),
        grid_spec=pltpu.PrefetchScalarGridSpec(
            num_scalar_prefetch=1, grid=(dec.steps,), in_specs=in_specs, out_specs=out_spec,
            scratch_shapes=scratch),
        out_shape=out_shape,
        compiler_params=_cparams("arbitrary"),
        name="fox_decode",
    )(page_table, *dec_ops)


def _out_kernel(a_ref, b_ref, wa_ref, wb_ref, x_ref, gate_ref, y_ref):
    out = _dot(a_ref[...], wa_ref[...]) + _dot(b_ref[...], wb_ref[...])
    y_ref[...] = x_ref[...] + gate_ref[...] * out


def _out_call(a, b, w_out_bf, x2, mod3, tm, tiles_per_group):
    m, d = x2.shape
    da, db_ = a.shape[1], b.shape[1]
    assert da == db_, "w_out is read as two equal row blocks"
    return pl.pallas_call(
        _out_kernel,
        grid=(m // tm,),
        in_specs=[pl.BlockSpec((tm, da), lambda i: (i, 0)),
                  pl.BlockSpec((tm, db_), lambda i: (i, 0)),
                  pl.BlockSpec((da, d), lambda i: (0, 0)),
                  pl.BlockSpec((db_, d), lambda i: (1, 0)),
                  pl.BlockSpec((tm, d), lambda i: (i, 0)),
                  _mod_spec(mod3, d, 2, tiles_per_group)],
        out_specs=pl.BlockSpec((tm, d), lambda i: (i, 0)),
        out_shape=jax.ShapeDtypeStruct((m, d), F32),
        compiler_params=_cparams("parallel"),
        name="out_proj",
    )(a, b, w_out_bf, w_out_bf, x2, mod3)


def kernel(x_prompt, x_sample, c_prompt, c_sample, cache_k, cache_v, cache_logf, state_hgrn,
           page_table, norm_w, w_ada, b_ada, w_in, b_fox_f, lb_logits, q_norm_w, k_norm_w,
           hgrn_norm_w, w_out):
    batch, seq, d = x_prompt.shape
    db, dseq, _ = x_sample.shape
    assert dseq == 1, "the sample group is one new row per sequence"
    depth, n_phys, page_rows, fheads, hd = cache_k.shape
    _, _, hheads, dk, dv = state_hgrn.shape
    hk, dh, dfox = hheads * dk, hheads * dv, fheads * hd
    n_seg = 2 * hk + 2 * dh + 4 * dfox
    assert w_in.shape[2] == n_seg + fheads
    mp, ms = batch * seq, db * dseq
    tm = _largest_tile(seq, ROW_TILE, LANES)

    xp = x_prompt.reshape(mp, d)
    xs = x_sample.reshape(ms, d)
    r_pad = -(batch + db) % SUBLANES
    c_all = jnp.concatenate([c_prompt, c_sample, jnp.zeros((r_pad, d), F32)], axis=0)

    outs = [[] for _ in range(8)]
    for l in range(depth):
        mod = _ada_call(c_all, w_ada[l], b_ada[l])
        mod_p = mod[:batch].reshape(batch, 1, 3 * d)
        mod_s = mod[batch:batch + db].reshape(1, db, 3 * d)
        w_all = w_in[l].T
        w_ff = jnp.pad(w_all[n_seg:], ((0, LANES - fheads), (0, 0)))
        b_ff = jnp.pad(b_fox_f[l].astype(F32), (0, LANES - fheads)).reshape(1, LANES)
        w_out_bf = w_out[l].astype(BF16)
        proj = functools.partial(_in_proj, norm_w=norm_w[l], w_all=w_all, w_ff=w_ff, b_ff=b_ff,
                                 lb_logits=lb_logits, q_norm_w=q_norm_w[l], k_norm_w=k_norm_w[l],
                                 layer=l, hgrn_heads=hheads, dk=dk, fox_heads=fheads, hd=hd)

        pp = proj(xp, mod3=mod_p, tiles_per_group=seq // tm, tm=tm, seq_len=seq, with_vt=True)
        ps = proj(xs, mod3=mod_s, tiles_per_group=1, tm=ms, seq_len=1)

        page_bias = _page_bias_call(cache_logf[l].reshape(n_phys, page_rows * fheads), fheads)
        n_pg = page_table.shape[1]
        pps = _largest_tile(n_pg, PAGES_PER_STEP, 1)
        pps_h = _largest_tile(n_pg, HGRN_PAGES_PER_STEP, 1)
        fox_steps, hgrn_steps = _fox_steps(batch, seq, fheads, tm), _hgrn_steps(batch, seq)
        n_fox = min(db - db // 2, fox_steps // (n_pg // pps))
        n_hgrn = min(db - n_fox, hgrn_steps // (n_pg // pps_h))
        n_rest = db - n_fox - n_hgrn
        plan = lambda seq0, nseq, steps, pages: None if nseq == 0 else _Dec(
            layer=l, seq0=seq0, nseq=nseq, n_pg=n_pg, pps=pages, rows=page_rows, heads=fheads,
            hd=hd, steps=steps)
        dec_fox = plan(0, n_fox, fox_steps, pps)
        dec_hgrn = plan(n_fox, n_hgrn, hgrn_steps, pps_h)
        dec_rest = plan(n_fox + n_hgrn, n_rest, n_rest * (n_pg // pps), pps)
        hosted = lambda dec: () if dec is None else tuple(
            _decode_operands(ps, cache_k, cache_v, page_bias, dec.pps, fheads, hd))

        a_p, s_p, *o_hgrn = _hgrn_prompt_call(pp, hgrn_norm_w[l], batch, seq, hheads, dk, dv,
                                              page_table, dec_hgrn, hosted(dec_hgrn))
        b_p, *o_fox = _fox_prompt_call(pp, batch, seq, fheads, hd, tm, page_table, dec_fox,
                                       hosted(dec_fox))
        xp = _out_call(a_p, b_p, w_out_bf, xp, mod_p, tm, seq // tm)
        outs[0].append(pp["f_k"].reshape(batch, seq, fheads, hd))
        outs[1].append(pp["f_v"].reshape(batch, seq, fheads, hd))
        outs[2].append(pp["f_logf"].reshape(batch, seq, fheads))
        outs[3].append(s_p)

        a_s, s_s = _hgrn_step_call(ps, hgrn_norm_w[l], state_hgrn[l], hheads, dk, dv)
        o_rest = [] if dec_rest is None else [_decode_call(dec_rest, hosted(dec_rest), page_table)]
        b_s = jnp.concatenate(o_fox + o_hgrn + o_rest, axis=0)[:, :fheads, :].reshape(db, dfox)
        xs = _out_call(a_s, b_s, w_out_bf, xs, mod_s, ms, 1)
        outs[4].append(ps["f_k"].reshape(db, dseq, fheads, hd))
        outs[5].append(ps["f_v"].reshape(db, dseq, fheads, hd))
        outs[6].append(ps["f_logf"].reshape(db, dseq, fheads))
        outs[7].append(s_s)

    stk = [jnp.stack(o) for o in outs]
    return (xp.reshape(batch, seq, d), xs.reshape(db, dseq, d), *stk)
```

```python
import functools
from typing import NamedTuple

import jax
import jax.numpy as jnp
from jax import lax
from jax.experimental import pallas as pl
from jax.experimental.pallas import tpu as pltpu

F32 = jnp.float32
BF16 = jnp.bfloat16

LANES = 128
SUBLANES = 8
VMEM_LIMIT = 56 * 1024 * 1024
RMS_EPS = 1e-6
LOG2E = 1.4426950408889634
HGRN_CHUNK = 128
ROW_TILE = 512
SEG_ROW_TILE = 1024
PAGES_PER_STEP = 8
FOX_HEADS_PER_STEP = 1
HGRN_PAGES_PER_STEP = 16
BIAS_PAGE_TILE = 256
HEAD_PAD = 16


def _cparams(*sem):
    return pltpu.CompilerParams(dimension_semantics=sem, vmem_limit_bytes=VMEM_LIMIT)


def _dot(a, b):
    return jnp.dot(a, b, preferred_element_type=F32)


def _dot_nt(a, b):
    return lax.dot_general(a, b, (((1,), (1,)), ((), ())), preferred_element_type=F32)


def _dot_tn(a, b):
    return lax.dot_general(a, b, (((0,), (0,)), ((), ())), preferred_element_type=F32)


def _split3(x):
    hi = x.astype(BF16)
    r = x - hi.astype(F32)
    mid = r.astype(BF16)
    lo = (r - mid.astype(F32)).astype(BF16)
    return hi, mid, lo


def _dot01(m01, x):
    hi, mid, lo = _split3(x)
    return _dot(m01, hi) + _dot(m01, mid) + _dot(m01, lo)


def _dot01_nt(x, m01):
    hi, mid, lo = _split3(x)
    return _dot(hi, m01) + _dot(mid, m01) + _dot(lo, m01)


def _sigmoid(x):
    return 1.0 / (1.0 + jnp.exp(-x))


def _silu(x):
    return x * _sigmoid(x)


def _rms(x, w):
    ms = jnp.mean(x * x, axis=-1, keepdims=True)
    return x * lax.rsqrt(ms + RMS_EPS) * w


def _largest_tile(n, cap, align):
    for t in range(min(cap, n), 0, -1):
        if n % t == 0 and t % align == 0:
            return t
    return n


def _ada_kernel(c_ref, w_ref, b_ref, o_ref):
    s = _silu(c_ref[...])
    s_hi = s.astype(BF16)
    s_lo = (s - s_hi.astype(F32)).astype(BF16)
    w = w_ref[...]
    w_hi = w.astype(BF16)
    w_lo = (w - w_hi.astype(F32)).astype(BF16)
    o_ref[...] = _dot(s_hi, w_hi) + _dot(s_hi, w_lo) + _dot(s_lo, w_hi) + b_ref[...]


def _ada_call(c_all, w_ada, b_ada):
    r, d = c_all.shape
    n = w_ada.shape[1]
    tn = 512
    return pl.pallas_call(
        _ada_kernel,
        grid=(n // tn,),
        in_specs=[pl.BlockSpec((r, d), lambda j: (0, 0)),
                  pl.BlockSpec((d, tn), lambda j: (0, j)),
                  pl.BlockSpec((1, tn), lambda j: (0, j))],
        out_specs=pl.BlockSpec((r, tn), lambda j: (0, j)),
        out_shape=jax.ShapeDtypeStruct((r, n), F32),
        compiler_params=_cparams("parallel"),
        name="ada_mod",
    )(c_all, w_ada, b_ada.reshape(1, n))


def _mod_spec(mod3, d, col, tiles_per_group):
    return pl.BlockSpec((None, mod3.shape[1], d), lambda i: (i // tiles_per_group, 0, col))


def _proj_first_kernel(x_ref, nw_ref, shift_ref, scale_ref, w_ref, wf_ref, bf_ref,
                       h_ref, hq_ref, logf_ref, c_ref, carry_ref, wbf_ref, *, heads, tiles_per_seq):
    _weights_once(w_ref, wbf_ref)
    xn = _rms(x_ref[...], nw_ref[...])
    h = (xn * (1.0 + scale_ref[...]) + shift_ref[...]).astype(BF16)
    h_ref[...] = h
    hq_ref[...] = _dot_nt(h, wbf_ref[...]).astype(BF16)
    x = _dot_nt(h, wf_ref[...].astype(BF16)) + bf_ref[...]
    lf = jnp.minimum(x, 0.0) - jnp.log1p(jnp.exp(-jnp.abs(x)))
    logf_ref[...] = lf[:, :heads]
    if tiles_per_seq is None:
        c_ref[...] = lf[:, :heads]
        return
    tm = lf.shape[0]

    @pl.when(pl.program_id(0) % tiles_per_seq == 0)
    def _():
        carry_ref[...] = jnp.zeros_like(carry_ref)

    row = lax.broadcasted_iota(jnp.int32, (LANES, LANES), 0)
    col = lax.broadcasted_iota(jnp.int32, (LANES, LANES), 1)
    tril = jnp.where(col <= row, 1.0, 0.0).astype(BF16)
    carry = carry_ref[...]
    for i in range(tm // LANES):
        c = _dot01(tril, lf[i * LANES:(i + 1) * LANES]) + carry
        c_ref[i * LANES:(i + 1) * LANES, :] = c[:, :heads]
        carry = c[LANES - 1:LANES, :]
    carry_ref[...] = carry


def _proj_first_call(x2, norm_w, mod3, tiles_per_group, w_all, seg_w, w_ff, b_ff, heads, tm, seq_len):
    m, d = x2.shape
    row = lambda width: pl.BlockSpec((tm, width), lambda i: (i, 0))
    tiles_per_seq = None if seq_len == 1 else seq_len // tm
    return pl.pallas_call(
        functools.partial(_proj_first_kernel, heads=heads, tiles_per_seq=tiles_per_seq),
        grid=(m // tm,),
        in_specs=[row(d), pl.BlockSpec((1, d), lambda i: (0, 0)),
                  _mod_spec(mod3, d, 0, tiles_per_group), _mod_spec(mod3, d, 1, tiles_per_group),
                  pl.BlockSpec((seg_w, d), lambda i: (0, 0)),
                  pl.BlockSpec(w_ff.shape, lambda i: (0, 0)), pl.BlockSpec(b_ff.shape, lambda i: (0, 0))],
        out_specs=(row(d), row(seg_w), row(heads), row(heads)),
        out_shape=(jax.ShapeDtypeStruct((m, d), BF16), jax.ShapeDtypeStruct((m, seg_w), BF16),
                   jax.ShapeDtypeStruct((m, heads), F32), jax.ShapeDtypeStruct((m, heads), F32)),
        scratch_shapes=[pltpu.VMEM((1, LANES), F32), pltpu.VMEM((seg_w, d), BF16)],
        compiler_params=_cparams("arbitrary"),
        name="proj_first",
    )(x2, norm_w.reshape(1, d), mod3, mod3, w_all, w_ff, b_ff)


def _weights_once(w_ref, wbf_ref):
    @pl.when(pl.program_id(0) == 0)
    def _():
        wbf_ref[...] = w_ref[...].astype(BF16)


def _project(h_ref, w_ref, wbf_ref):
    _weights_once(w_ref, wbf_ref)
    return _dot_nt(h_ref[...], wbf_ref[...])


def _proj_plain_kernel(h_ref, w_ref, o_ref, wbf_ref, *, act):
    z = _project(h_ref, w_ref, wbf_ref)
    if act:
        z = _silu(z)
    o_ref[...] = z.astype(o_ref.dtype)


def _proj_v_kernel(h_ref, w_ref, o_ref, wbf_ref):
    o_ref[...] = _project(h_ref, w_ref, wbf_ref)


def _proj_vt_kernel(h_ref, w_ref, o_ref, ot_ref, wbf_ref):
    z = _project(h_ref, w_ref, wbf_ref)
    o_ref[...] = z
    zt = z.T.astype(BF16)
    t = ot_ref.shape[2]
    for u in range(ot_ref.shape[0]):
        ot_ref[u] = zt[:, u * t:(u + 1) * t]


def _proj_forget_kernel(h_ref, w_ref, lbl_ref, logf_ref, k_ref, wbf_ref, *, layer):
    z = _project(h_ref, w_ref, wbf_ref)
    lbl = lbl_ref[...]
    e = jnp.exp(lbl - jnp.max(lbl, axis=0, keepdims=True))
    lb = jnp.sum(e[:layer + 1], axis=0, keepdims=True) / jnp.sum(e, axis=0, keepdims=True)
    t = jnp.exp(-jnp.abs(z))
    big = 1.0 / (1.0 + t)
    small = t * big
    pos = z >= 0.0
    logf_ref[...] = jnp.log(lb + (1.0 - lb) * jnp.where(pos, big, small))
    k_ref[...] = ((1.0 - lb) * jnp.where(pos, small, big)).astype(BF16)


def _aug_lanes(c2, is_query, tm):
    hi, mid, lo = _split3(c2)
    hi, mid, lo = hi.astype(F32), mid.astype(F32), lo.astype(F32)
    lane = lax.broadcasted_iota(jnp.int32, (tm, LANES), 1)
    one = jnp.ones((tm, LANES), F32)
    zero = jnp.zeros((tm, LANES), F32)
    if is_query:
        parts = jnp.where(lane == 0, hi, jnp.where(lane == 1, mid, jnp.where(lane == 2, lo, zero)))
        return jnp.where((lane >= 3) & (lane < 6), one, parts).astype(BF16)
    parts = jnp.where(lane == 3, -hi, jnp.where(lane == 4, -mid, jnp.where(lane == 5, -lo, zero)))
    return jnp.where(lane < 3, one, parts).astype(BF16)


def _proj_qk_kernel(h_ref, w_ref, nw_ref, c_ref, *refs, heads, hd, is_query, qscale):
    out_refs, wbf_ref = refs[:-1], refs[-1]
    z = _project(h_ref, w_ref, wbf_ref)
    tm = z.shape[0]
    nw = nw_ref[...]
    c = c_ref[...]
    for h in range(heads):
        n = _rms(z[:, h * hd:(h + 1) * hd], nw)
        aug = _aug_lanes(c[:, h:h + 1] * LOG2E, is_query, tm)
        if is_query:
            (aug_ref,) = out_refs
            n = n * qscale
        else:
            full_ref, aug_ref = out_refs
            full_ref[:, h * hd:(h + 1) * hd] = n
        aug_ref[:, 2 * h * hd:(2 * h + 1) * hd] = n.astype(BF16)
        aug_ref[:, (2 * h + 1) * hd:(2 * h + 2) * hd] = aug


def _proj_call(kernel, h, w, seg_w, seg, extra_inputs, extra_specs, out_shapes, out_specs, tm,
               name="in_proj"):
    m, d = h.shape
    return pl.pallas_call(
        kernel,
        grid=(m // tm,),
        in_specs=[pl.BlockSpec((tm, d), lambda i: (i, 0)),
                  pl.BlockSpec((seg_w, d), lambda i: (seg, 0))] + list(extra_specs),
        out_specs=out_specs,
        out_shape=out_shapes,
        scratch_shapes=[pltpu.VMEM((seg_w, d), BF16)],
        compiler_params=_cparams("arbitrary"),
        name=name,
    )(h, w, *extra_inputs)


def _full_spec(shape):
    nd = len(shape)
    return pl.BlockSpec(shape, lambda i: (0,) * nd)


def _in_proj(x2, norm_w, mod3, tiles_per_group, w_all, w_ff, b_ff, lb_logits, q_norm_w, k_norm_w,
             layer, tm, seq_len, hgrn_heads, dk, fox_heads, hd, with_vt=False):
    m, d = x2.shape
    hk = hgrn_heads * dk
    dfox = fox_heads * hd
    row = lambda width: pl.BlockSpec((tm, width), lambda i: (i, 0))
    sds = lambda width, dt: jax.ShapeDtypeStruct((m, width), dt)
    assert hk == dfox, "segments are indexed as equal-width column blocks"
    h, hq, f_logf, f_c = _proj_first_call(x2, norm_w, mod3, tiles_per_group, w_all, hk, w_ff, b_ff,
                                          fox_heads, tm, seq_len)
    t_attn = tm
    tm = _largest_tile(m, SEG_ROW_TILE, tm)
    row = lambda width: pl.BlockSpec((tm, width), lambda i: (i, 0))
    plain = lambda seg, act, nm: _proj_call(
        functools.partial(_proj_plain_kernel, act=act), h, w_all, hk, seg, (), (),
        sds(hk, BF16), row(hk), tm, name=nm)
    a_logf, a_k = _proj_call(
        functools.partial(_proj_forget_kernel, layer=layer), h, w_all, hk, 1,
        (lb_logits,), (_full_spec(lb_logits.shape),),
        (sds(hk, F32), sds(hk, BF16)), (row(hk), row(hk)), tm, name="proj_hf")
    hi = plain(2, False, "proj_hi")
    sg_h = plain(3, True, "proj_hg")
    qk = lambda seg, nw, is_query, outs, specs, nm: _proj_call(
        functools.partial(_proj_qk_kernel, heads=fox_heads, hd=hd, is_query=is_query,
                          qscale=hd ** -0.5 * LOG2E),
        h, w_all, dfox, seg, (nw.reshape(1, hd), f_c), (_full_spec((1, hd)), row(fox_heads)),
        outs, specs, tm, name=nm)
    q_aug = qk(4, q_norm_w, True, sds(2 * dfox, BF16), row(2 * dfox), "proj_fq")
    f_k, k_aug = qk(5, k_norm_w, False, (sds(dfox, F32), sds(2 * dfox, BF16)),
                    (row(dfox), row(2 * dfox)), "proj_fk")
    if not with_vt:
        f_v = _proj_call(_proj_v_kernel, h, w_all, dfox, 6, (), (), sds(dfox, F32), row(dfox), tm,
                         name="proj_fv")
        v_t = None
    else:
        f_v, v_t = _proj_call(
            _proj_vt_kernel, h, w_all, dfox, 6, (), (),
            (sds(dfox, F32), jax.ShapeDtypeStruct((m // t_attn, dfox, t_attn), BF16)),
            (row(dfox), pl.BlockSpec((tm // t_attn, dfox, t_attn), lambda i: (i, 0, 0))), tm,
            name="proj_fv")
    sg_f = plain(7, True, "proj_fg")
    return dict(hq=hq, a_logf=a_logf, a_k=a_k, hi=hi, sg_h=sg_h, f_logf=f_logf, f_c=f_c,
                q_aug=q_aug, f_k=f_k, k_aug=k_aug, f_v=f_v, v_t=v_t, sg_f=sg_f)


def _hgrn_levels(c):
    levels = []
    m = c // 2
    while m >= 1:
        levels.append(m)
        m //= 2
    return levels


def _hgrn_level_exponents(b, g, c):
    dk = b.shape[1]
    ridx = lax.broadcasted_iota(jnp.int32, (c, dk), 0)
    out = []
    for m in _hgrn_levels(c):
        pos = ridx & (2 * m - 1)
        later = pos >= m
        if 2 * m >= SUBLANES:
            b3 = b.reshape(c // (2 * m), 2 * m, dk)
            bm = jnp.broadcast_to(b3[:, m - 1:m, :], b3.shape).reshape(c, dk)
            e = jnp.where(later, b - bm, bm - b)
        elif m == 2:
            g_next = pltpu.roll(g, c - 1, 0)
            g_prev = pltpu.roll(g, 1, 0)
            e = jnp.where(pos == 0, g_next,
                          jnp.where(pos == 2, g, jnp.where(pos == 3, g + g_prev, 0.0)))
        else:
            e = jnp.where(later, g, 0.0)
        out.append((e, later))
    return out


def _hgrn_kernel(pt_ref, q_ref, k_ref, g_ref, v_ref, sg_ref, nw_ref, *rest, c, heads, dk, dv, dec):
    ci = pl.program_id(1)
    if dec is None:
        o_ref, s_ref, st_ref = rest
    else:
        n_in = _dec_num_inputs(dec)
        (o_ref, s_ref, do_ref), st_ref = rest[n_in:n_in + 3], rest[n_in + 3]
        lin = pl.program_id(0) * pl.num_programs(1) + ci
        dec_args = (dec, pt_ref, lin, rest[:n_in], do_ref, rest[n_in + 4:])

    @pl.when(ci == 0)
    def _():
        st_ref[...] = jnp.zeros_like(st_ref)

    dec_state = None if dec is None else _dec_main(*dec_args)

    row = lax.broadcasted_iota(jnp.int32, (c, c), 0)
    col = lax.broadcasted_iota(jnp.int32, (c, c), 1)
    tril = jnp.where(col <= row, 1.0, 0.0).astype(BF16)
    eye = row == col
    masks = []
    for m in _hgrn_levels(c):
        same = (row // (2 * m)) == (col // (2 * m))
        masks.append(same & ((row & (2 * m - 1)) >= m) & ((col & (2 * m - 1)) < m))
    nw = nw_ref[...]

    for h in range(heads):
        ks = slice(h * dk, (h + 1) * dk)
        vs = slice(h * dv, (h + 1) * dv)
        q = q_ref[:, ks].astype(F32)
        k = k_ref[:, ks].astype(F32)
        g = g_ref[:, ks] * LOG2E
        v = v_ref[:, vs]
        b = _dot01(tril, g)
        b_last = b[c - 1:c]
        a = jnp.where(eye, _dot_nt(q.astype(BF16), k.astype(BF16)), 0.0)
        for (e, later), mask in zip(_hgrn_level_exponents(b, g, c), masks):
            x = (jnp.where(later, q, k) * jnp.exp2(e)).astype(BF16)
            a = jnp.where(mask, _dot_nt(x, x), a)
        st = st_ref[h]
        o = _dot_nt((q * jnp.exp2(b)).astype(BF16), st.astype(BF16)) + _dot(a.astype(BF16), v)
        kd = (k * jnp.exp2(b_last - b)).astype(BF16)
        st_ref[h] = st * jnp.exp2(b_last) + _dot_tn(v, kd)
        o_ref[:, vs] = (_rms(o, nw) * sg_ref[:, vs].astype(F32)).astype(BF16)

    @pl.when(ci == pl.num_programs(1) - 1)
    def _():
        for h in range(heads):
            s_ref[h] = st_ref[h].T

    if dec is not None:
        _dec_finish(*dec_args, dec_state)


def _hgrn_steps(batch, seq):
    return batch * (seq // HGRN_CHUNK)


def _hgrn_prompt_call(p, hgrn_norm_w, batch, seq, heads, dk, dv, page_table, dec=None, dec_ops=()):
    c = HGRN_CHUNK
    nc = seq // c
    blk = lambda width: pl.BlockSpec((c, heads * width), lambda b, i, pt: (b * nc + i, 0))
    in_specs = [blk(dk), blk(dk), blk(dk), blk(dv), blk(dv),
                pl.BlockSpec((1, dv), lambda b, i, pt: (0, 0))]
    out_specs = [blk(dv), pl.BlockSpec((None, heads, dk, dv), lambda b, i, pt: (b, 0, 0, 0))]
    out_shape = [jax.ShapeDtypeStruct((batch * seq, heads * dv), BF16),
                 jax.ShapeDtypeStruct((batch, heads, dk, dv), F32)]
    scratch = [pltpu.VMEM((heads, dv, dk), F32)]
    if dec is not None:
        d_in, d_out, d_shape, d_scratch = _dec_specs(dec, lambda b, i: b * nc + i)
        in_specs += d_in
        out_specs.append(d_out)
        out_shape.append(d_shape)
        scratch += d_scratch
    return pl.pallas_call(
        functools.partial(_hgrn_kernel, c=c, heads=heads, dk=dk, dv=dv, dec=dec),
        grid_spec=pltpu.PrefetchScalarGridSpec(
            num_scalar_prefetch=1, grid=(batch, nc), in_specs=in_specs, out_specs=out_specs,
            scratch_shapes=scratch),
        out_shape=out_shape,
        compiler_params=_cparams("parallel" if dec is None else "arbitrary", "arbitrary"),
        name="hgrn_prompt",
    )(page_table, p["hq"], p["a_k"], p["a_logf"], p["hi"], p["sg_h"], hgrn_norm_w.reshape(1, dv),
      *dec_ops)


def _hgrn_step_kernel(q_ref, k_ref, g_ref, v_ref, sg_ref, nw_ref, s0_ref, o_ref, s_ref, *, heads):
    pad = lambda x: jnp.concatenate([x, jnp.zeros((LANES - heads, x.shape[1]), F32)], axis=0)
    qt = pad(q_ref[...].astype(F32)).T
    kt = pad(k_ref[...].astype(F32)).T
    ft = pad(jnp.exp(g_ref[...])).T
    v = v_ref[...].astype(F32)
    sg = sg_ref[...].astype(F32)
    rows = []
    for h in range(heads):
        s = ft[:, h:h + 1] * s0_ref[h] + kt[:, h:h + 1] * v[h:h + 1, :]
        s_ref[h] = s
        o = jnp.sum(qt[:, h:h + 1] * s, axis=0, keepdims=True)
        rows.append(_rms(o, nw_ref[...]) * sg[h:h + 1, :])
    o_ref[...] = jnp.concatenate(rows, axis=0).astype(BF16)


def _hgrn_step_call(p, hgrn_norm_w, state, heads, dk, dv):
    db = state.shape[0]
    r3 = lambda x, w: x.reshape(db, heads, w)
    blk = lambda w: pl.BlockSpec((None, heads, w), lambda b: (b, 0, 0))
    st_blk = pl.BlockSpec((None, heads, dk, dv), lambda b: (b, 0, 0, 0))
    o, s = pl.pallas_call(
        functools.partial(_hgrn_step_kernel, heads=heads),
        grid=(db,),
        in_specs=[blk(dk), blk(dk), blk(dk), blk(dv), blk(dv),
                  pl.BlockSpec((1, dv), lambda b: (0, 0)), st_blk],
        out_specs=(blk(dv), st_blk),
        out_shape=(jax.ShapeDtypeStruct((db, heads, dv), BF16),
                   jax.ShapeDtypeStruct(state.shape, F32)),
        compiler_params=_cparams("parallel"),
        name="hgrn_step",
    )(r3(p["hq"], dk), r3(p["a_k"], dk), r3(p["a_logf"], dk), r3(p["hi"], dv), r3(p["sg_h"], dv),
      hgrn_norm_w.reshape(1, dv), state)
    return o.reshape(db, heads * dv), s


def _fox_kernel(pt_ref, q_ref, k_ref, vt_ref, sg_ref, *rest, t, hpb, dec):
    qi = pl.program_id(2)
    if dec is None:
        o_ref, s_ref, m_ref, l_ref, acc_ref = rest
        dec_state = None
    else:
        n_in = _dec_num_inputs(dec)
        o_ref, do_ref, s_ref, m_ref, l_ref, acc_ref = rest[n_in:n_in + 6]
        lin = ((pl.program_id(0) * pl.num_programs(1) + pl.program_id(1)) * pl.num_programs(2) + qi)
        dec_args = (dec, pt_ref, lin, rest[:n_in], do_ref, rest[n_in + 6:])
        dec_state = _dec_main(*dec_args)
    hd = acc_ref.shape[1]
    units = range(hpb)
    m_ref[...] = jnp.full_like(m_ref, -jnp.inf)
    l_ref[...] = jnp.zeros_like(l_ref)
    acc_ref[...] = jnp.zeros_like(acc_ref)

    cols = [slice(2 * u * hd, 2 * (u + 1) * hd) for u in units]
    qs = [q_ref[:, c] for c in cols]

    def scores(j, u):
        return _dot_nt(k_ref[pl.ds(pl.multiple_of(j * t, t), t), cols[u]], qs[u])

    def update(j, u, s):
        m_prev = m_ref[u]
        m_new = jnp.maximum(m_prev, jnp.max(s, axis=0, keepdims=True))
        alpha = jnp.exp2(m_prev - m_new)
        p = jnp.exp2(s - m_new)
        l_ref[u] = alpha * l_ref[u] + jnp.sum(p, axis=0, keepdims=True)
        acc_ref[u] = alpha * acc_ref[u] + _dot(vt_ref[j, u * hd:(u + 1) * hd, :], p.astype(BF16))
        m_ref[u] = m_new

    for u in units:
        s_ref[u] = scores(0, u)

    def body(j, carry):
        s_cur = [s_ref[u] for u in units]
        s_next = [scores(j + 1, u) for u in units]
        for u in units:
            update(j, u, s_cur[u])
        for u in units:
            s_ref[u] = s_next[u]
        return carry

    lax.fori_loop(0, qi, body, 0)
    key = lax.broadcasted_iota(jnp.int32, (t, t), 0)
    qry = lax.broadcasted_iota(jnp.int32, (t, t), 1)
    for u in units:
        update(qi, u, jnp.where(key <= qry, s_ref[u], -jnp.inf))
    for u in units:
        o = (acc_ref[u] / l_ref[u]).T
        o_ref[:, u * hd:(u + 1) * hd] = (o * sg_ref[:, u * hd:(u + 1) * hd].astype(F32)).astype(BF16)
    if dec is not None:
        _dec_finish(*dec_args, dec_state)


def _fox_heads_per_step(heads):
    return _largest_tile(heads, FOX_HEADS_PER_STEP, 1)


def _fox_steps(batch, seq, heads, t):
    return batch * (heads // _fox_heads_per_step(heads)) * (seq // t)


def _fox_prompt_call(p, batch, seq, heads, hd, t, page_table, dec=None, dec_ops=()):
    nq = seq // t
    hpb = _fox_heads_per_step(heads)
    hg = heads // hpb
    in_specs = [pl.BlockSpec((t, 2 * hd * hpb), lambda b, h, i, pt: (b * nq + i, h)),
                pl.BlockSpec((seq, 2 * hd * hpb), lambda b, h, i, pt: (b, h)),
                pl.BlockSpec((nq, hd * hpb, t), lambda b, h, i, pt: (b, h, 0)),
                pl.BlockSpec((t, hd * hpb), lambda b, h, i, pt: (b * nq + i, h))]
    out_specs = [pl.BlockSpec((t, hd * hpb), lambda b, h, i, pt: (b * nq + i, h))]
    out_shape = [jax.ShapeDtypeStruct((batch * seq, heads * hd), BF16)]
    scratch = [pltpu.VMEM((hpb, t, t), F32), pltpu.VMEM((hpb, 1, t), F32),
               pltpu.VMEM((hpb, 1, t), F32), pltpu.VMEM((hpb, hd, t), F32)]
    if dec is not None:
        d_in, d_out, d_shape, d_scratch = _dec_specs(dec, lambda b, h, i: (b * hg + h) * nq + i)
        in_specs += d_in
        out_specs.append(d_out)
        out_shape.append(d_shape)
        scratch += d_scratch
    return pl.pallas_call(
        functools.partial(_fox_kernel, t=t, hpb=hpb, dec=dec),
        grid_spec=pltpu.PrefetchScalarGridSpec(
            num_scalar_prefetch=1, grid=(batch, hg, nq), in_specs=in_specs, out_specs=out_specs,
            scratch_shapes=scratch),
        out_shape=out_shape,
        compiler_params=_cparams(*(["parallel" if dec is None else "arbitrary"] * 2), "arbitrary"),
        name="fox_prompt",
    )(page_table, p["q_aug"], p["k_aug"], p["v_t"], p["sg_f"], *dec_ops)


def _page_bias_kernel(x_ref, u_ref, o_ref, *, heads):
    x = x_ref[...]
    n = x.shape[1]
    o_ref[:, :n] = _dot01_nt(x, u_ref[...])
    t = x[:, :LANES]
    for i in range(1, n // LANES):
        t = t + x[:, i * LANES:(i + 1) * LANES]
    shift = heads
    while shift < LANES:
        t = t + pltpu.roll(t, shift, 1)
        shift *= 2
    o_ref[:, n:] = jnp.concatenate([t] * (n // LANES), axis=1)


def _page_bias_call(lf_flat, heads):
    n_pages, n = lf_flat.shape
    assert LANES % heads == 0 and n % LANES == 0
    idx = jnp.arange(n, dtype=jnp.int32)
    later = (idx[:, None] // heads) > (idx[None, :] // heads)
    same_head = (idx[:, None] % heads) == (idx[None, :] % heads)
    u = (later & same_head).astype(BF16)
    pb = _largest_tile(n_pages, BIAS_PAGE_TILE, SUBLANES)
    return pl.pallas_call(
        functools.partial(_page_bias_kernel, heads=heads),
        grid=(n_pages // pb,),
        in_specs=[pl.BlockSpec((pb, n), lambda i: (i, 0)), pl.BlockSpec((n, n), lambda i: (0, 0))],
        out_specs=pl.BlockSpec((pb, 2 * n), lambda i: (i, 0)),
        out_shape=jax.ShapeDtypeStruct((n_pages, 2 * n), F32),
        compiler_params=_cparams("parallel"),
        name="page_bias",
    )(lf_flat, u)


class _Dec(NamedTuple):
    layer: int
    seq0: int
    nseq: int
    n_pg: int
    pps: int
    rows: int
    heads: int
    hd: int
    steps: int

    @property
    def groups(self):
        return self.n_pg // self.pps

    @property
    def guarded(self):
        return self.steps != self.nseq * self.groups


def _dec_where(dec, lin):
    active = lin < dec.nseq * dec.groups
    local = jnp.minimum(lin // dec.groups, dec.nseq - 1)
    grp = jnp.where(active, lin % dec.groups, dec.groups - 1)
    return active, local, grp


def _dec_num_inputs(dec):
    return 2 + 3 * dec.pps


_SEQ_Q, _SEQ_K, _SEQ_V, _SEQ_GATE = range(4)


def _dec_specs(dec, lin_of):
    heads, hd = dec.heads, dec.hd
    n = dec.rows * heads

    def seq_blk(shape, base):
        def imap(*a):
            return (base + _dec_where(dec, lin_of(*a[:-1]))[1], 0, 0)
        return pl.BlockSpec((None,) + shape, imap)

    def page(a, slot):
        _, local, grp = _dec_where(dec, lin_of(*a[:-1]))
        return a[-1][dec.seq0 + local, dec.n_pg - 1 - (grp * dec.pps + slot)]

    def cache_blk(slot):
        return pl.BlockSpec((None, None, dec.rows, heads, hd),
                            lambda *a: (dec.layer, page(a, slot), 0, 0, 0))

    def bias_blk(slot):
        return pl.BlockSpec((SUBLANES, 2 * n), lambda *a: (page(a, slot) // SUBLANES, 0))

    slots = range(dec.pps)
    in_specs = ([seq_blk((4 * HEAD_PAD, hd), dec.seq0), seq_blk((1, n), dec.seq0)]
                + [cache_blk(s) for s in slots] + [cache_blk(s) for s in slots]
                + [bias_blk(s) for s in slots])
    scratch = [pltpu.VMEM((HEAD_PAD, LANES), F32), pltpu.VMEM((HEAD_PAD, LANES), F32),
               pltpu.VMEM((1, n), F32), pltpu.VMEM((HEAD_PAD, hd), F32)]
    out_shape = jax.ShapeDtypeStruct((dec.nseq, HEAD_PAD, hd), BF16)
    return in_specs, seq_blk((HEAD_PAD, hd), 0), out_shape, scratch


def _seq_rows(seq_ref, which):
    return seq_ref[which * HEAD_PAD:(which + 1) * HEAD_PAD, :]


def _dec_pages(dec, pt_ref, local, grp, in_refs, scratch):
    pps, heads, hd = dec.pps, dec.heads, dec.hd
    seq_ref, cn_ref = in_refs[:2]
    k_refs = in_refs[2:2 + pps]
    v_refs = in_refs[2 + pps:2 + 2 * pps]
    b_refs = in_refs[2 + 2 * pps:2 + 3 * pps]
    m_ref, l_ref, run_ref, acc_ref = scratch
    n = dec.rows * heads

    @pl.when(grp == 0)
    def _():
        m_ref[...] = jnp.full_like(m_ref, -jnp.inf)
        l_ref[...] = jnp.zeros_like(l_ref)
        run_ref[...] = jnp.zeros_like(run_ref)
        acc_ref[...] = jnp.zeros_like(acc_ref)

    q = _seq_rows(seq_ref, _SEQ_Q).astype(BF16)
    sub = lax.broadcasted_iota(jnp.int32, (HEAD_PAD, n), 0)
    lane = lax.broadcasted_iota(jnp.int32, (HEAD_PAD, n), 1)
    own = (lane % heads) == (sub % heads)
    cn = cn_ref[...]
    run = run_ref[...]
    scores = []
    for p in range(pps):
        r = pt_ref[dec.seq0 + local, dec.n_pg - 1 - (grp * pps + p)] % SUBLANES
        kb = k_refs[p][...].reshape(n, hd).astype(BF16)
        bias = (b_refs[p][pl.ds(r, 1), :n] + run + cn) * LOG2E
        scores.append(jnp.where(own, _dot_nt(q, kb) + bias, -jnp.inf))
        run = run + b_refs[p][pl.ds(r, 1), n:]
    run_ref[...] = run

    m_prev = m_ref[:, :1]
    m_new = m_prev
    for s in scores:
        m_new = jnp.maximum(m_new, jnp.max(s, axis=-1, keepdims=True))
    alpha = jnp.exp2(m_prev - m_new)
    l_new = alpha * l_ref[:, :1]
    acc = alpha * acc_ref[...]
    for p in range(pps):
        pr = jnp.exp2(scores[p] - m_new)
        l_new = l_new + jnp.sum(pr, axis=-1, keepdims=True)
        acc = acc + _dot(pr.astype(BF16), v_refs[p][...].reshape(n, hd).astype(BF16))
    m_ref[...] = jnp.broadcast_to(m_new, m_ref.shape)
    l_ref[...] = jnp.broadcast_to(l_new, l_ref.shape)
    acc_ref[...] = acc
    return m_new, l_new, acc


def _dec_new_row(dec, grp, in_refs, o_ref, state):
    seq_ref = in_refs[0]
    m_new, l_new, acc = state

    @pl.when(grp == dec.groups - 1)
    def _():
        s_new = jnp.sum(_seq_rows(seq_ref, _SEQ_Q) * _seq_rows(seq_ref, _SEQ_K), axis=-1,
                        keepdims=True)
        m_fin = jnp.maximum(m_new, s_new)
        a_fin = jnp.exp2(m_new - m_fin)
        p_new = jnp.exp2(s_new - m_fin)
        l_fin = a_fin * l_new + p_new
        out = (a_fin * acc + p_new * _seq_rows(seq_ref, _SEQ_V)) / l_fin
        o_ref[...] = (out * _seq_rows(seq_ref, _SEQ_GATE)).astype(BF16)


def _dec_main(dec, pt_ref, lin, in_refs, o_ref, scratch):
    active, local, grp = _dec_where(dec, lin)
    if not dec.guarded:
        return _dec_pages(dec, pt_ref, local, grp, in_refs, scratch)

    @pl.when(active)
    def _():
        state = _dec_pages(dec, pt_ref, local, grp, in_refs, scratch)
        _dec_new_row(dec, grp, in_refs, o_ref, state)
    return None


def _dec_finish(dec, pt_ref, lin, in_refs, o_ref, scratch, state):
    del pt_ref, scratch
    if state is not None:
        _dec_new_row(dec, _dec_where(dec, lin)[2], in_refs, o_ref, state)


def _decode_kernel(pt_ref, *refs, dec):
    n_in = _dec_num_inputs(dec)
    args = (dec, pt_ref, pl.program_id(0), refs[:n_in], refs[n_in], refs[n_in + 1:])
    _dec_finish(*args, _dec_main(*args))


def _decode_operands(p, cache_k, cache_v, page_bias, pps, heads, hd):
    db = p["f_logf"].shape[0]
    rows = cache_k.shape[2]
    padh = lambda x: jnp.pad(x.astype(F32), ((0, 0), (0, HEAD_PAD - heads), (0, 0)))
    head_rows = lambda x: padh(x.reshape(db, heads, hd))
    main = lambda x: padh(x.reshape(db, heads, 2 * hd)[:, :, :hd])
    seq = jnp.concatenate([main(p["q_aug"]), main(p["k_aug"]), head_rows(p["f_v"]),
                           head_rows(p["sg_f"])], axis=1)
    cn = jnp.tile(p["f_logf"], (1, rows)).reshape(db, 1, rows * heads)
    return [seq, cn] + [cache_k] * pps + [cache_v] * pps + [page_bias] * pps


def _decode_call(dec, dec_ops, page_table):
    in_specs, out_spec, out_shape, scratch = _dec_specs(dec, lambda i: i)
    return pl.pallas_call(
        functools.partial(_decode_kernel, dec=dec),
        grid_spec=pltpu.PrefetchScalarGridSpec(
            num_scalar_prefetch=1, grid=(dec.steps,), in_specs=in_specs, out_specs=out_spec,
            scratch_shapes=scratch),
        out_shape=out_shape,
        compiler_params=_cparams("arbitrary"),
        name="fox_decode",
    )(page_table, *dec_ops)


def _out_kernel(a_ref, b_ref, wa_ref, wb_ref, x_ref, gate_ref, y_ref):
    out = _dot(a_ref[...], wa_ref[...]) + _dot(b_ref[...], wb_ref[...])
    y_ref[...] = x_ref[...] + gate_ref[...] * out


def _out_call(a, b, w_out_bf, x2, mod3, tm, tiles_per_group):
    m, d = x2.shape
    da, db_ = a.shape[1], b.shape[1]
    assert da == db_, "w_out is read as two equal row blocks"
    return pl.pallas_call(
        _out_kernel,
        grid=(m // tm,),
        in_specs=[pl.BlockSpec((tm, da), lambda i: (i, 0)),
                  pl.BlockSpec((tm, db_), lambda i: (i, 0)),
                  pl.BlockSpec((da, d), lambda i: (0, 0)),
                  pl.BlockSpec((db_, d), lambda i: (1, 0)),
                  pl.BlockSpec((tm, d), lambda i: (i, 0)),
                  _mod_spec(mod3, d, 2, tiles_per_group)],
        out_specs=pl.BlockSpec((tm, d), lambda i: (i, 0)),
        out_shape=jax.ShapeDtypeStruct((m, d), F32),
        compiler_params=_cparams("parallel"),
        name="out_proj",
    )(a, b, w_out_bf, w_out_bf, x2, mod3)


def kernel(x_prompt, x_sample, c_prompt, c_sample, cache_k, cache_v, cache_logf, state_hgrn,
           page_table, norm_w, w_ada, b_ada, w_in, b_fox_f, lb_logits, q_norm_w, k_norm_w,
           hgrn_norm_w, w_out):
    batch, seq, d = x_prompt.shape
    db, dseq, _ = x_sample.shape
    assert dseq == 1, "the sample group is one new row per sequence"
    depth, n_phys, page_rows, fheads, hd = cache_k.shape
    _, _, hheads, dk, dv = state_hgrn.shape
    hk, dh, dfox = hheads * dk, hheads * dv, fheads * hd
    n_seg = 2 * hk + 2 * dh + 4 * dfox
    assert w_in.shape[2] == n_seg + fheads
    mp, ms = batch * seq, db * dseq
    tm = _largest_tile(seq, ROW_TILE, LANES)

    xp = x_prompt.reshape(mp, d)
    xs = x_sample.reshape(ms, d)
    r_pad = -(batch + db) % SUBLANES
    c_all = jnp.concatenate([c_prompt, c_sample, jnp.zeros((r_pad, d), F32)], axis=0)

    outs = [[] for _ in range(8)]
    for l in range(depth):
        mod = _ada_call(c_all, w_ada[l], b_ada[l])
        mod_p = mod[:batch].reshape(batch, 1, 3 * d)
        mod_s = mod[batch:batch + db].reshape(1, db, 3 * d)
        w_all = w_in[l].T
        w_ff = jnp.pad(w_all[n_seg:], ((0, LANES - fheads), (0, 0)))
        b_ff = jnp.pad(b_fox_f[l].astype(F32), (0, LANES - fheads)).reshape(1, LANES)
        w_out_bf = w_out[l].astype(BF16)
        proj = functools.partial(_in_proj, norm_w=norm_w[l], w_all=w_all, w_ff=w_ff, b_ff=b_ff,
                                 lb_logits=lb_logits, q_norm_w=q_norm_w[l], k_norm_w=k_norm_w[l],
                                 layer=l, hgrn_heads=hheads, dk=dk, fox_heads=fheads, hd=hd)

        pp = proj(xp, mod3=mod_p, tiles_per_group=seq // tm, tm=tm, seq_len=seq, with_vt=True)
        ps = proj(xs, mod3=mod_s, tiles_per_group=1, tm=ms, seq_len=1)

        page_bias = _page_bias_call(cache_logf[l].reshape(n_phys, page_rows * fheads), fheads)
        n_pg = page_table.shape[1]
        pps = _largest_tile(n_pg, PAGES_PER_STEP, 1)
        pps_h = _largest_tile(n_pg, HGRN_PAGES_PER_STEP, 1)
        fox_steps, hgrn_steps = _fox_steps(batch, seq, fheads, tm), _hgrn_steps(batch, seq)
        n_fox = min(db - db // 2, fox_steps // (n_pg // pps))
        n_hgrn = min(db - n_fox, hgrn_steps // (n_pg // pps_h))
        n_rest = db - n_fox - n_hgrn
        plan = lambda seq0, nseq, steps, pages: None if nseq == 0 else _Dec(
            layer=l, seq0=seq0, nseq=nseq, n_pg=n_pg, pps=pages, rows=page_rows, heads=fheads,
            hd=hd, steps=steps)
        dec_fox = plan(0, n_fox, fox_steps, pps)
        dec_hgrn = plan(n_fox, n_hgrn, hgrn_steps, pps_h)
        dec_rest = plan(n_fox + n_hgrn, n_rest, n_rest * (n_pg // pps), pps)
        hosted = lambda dec: () if dec is None else tuple(
            _decode_operands(ps, cache_k, cache_v, page_bias, dec.pps, fheads, hd))

        a_p, s_p, *o_hgrn = _hgrn_prompt_call(pp, hgrn_norm_w[l], batch, seq, hheads, dk, dv,
                                              page_table, dec_hgrn, hosted(dec_hgrn))
        b_p, *o_fox = _fox_prompt_call(pp, batch, seq, fheads, hd, tm, page_table, dec_fox,
                                       hosted(dec_fox))
        xp = _out_call(a_p, b_p, w_out_bf, xp, mod_p, tm, seq // tm)
        outs[0].append(pp["f_k"].reshape(batch, seq, fheads, hd))
        outs[1].append(pp["f_v"].reshape(batch, seq, fheads, hd))
        outs[2].append(pp["f_logf"].reshape(batch, seq, fheads))
        outs[3].append(s_p)

        a_s, s_s = _hgrn_step_call(ps, hgrn_norm_w[l], state_hgrn[l], hheads, dk, dv)
        o_rest = [] if dec_rest is None else [_decode_call(dec_rest, hosted(dec_rest), page_table)]
        b_s = jnp.concatenate(o_fox + o_hgrn + o_rest, axis=0)[:, :fheads, :].reshape(db, dfox)
        xs = _out_call(a_s, b_s, w_out_bf, xs, mod_s, ms, 1)
        outs[4].append(ps["f_k"].reshape(db, dseq, fheads, hd))
        outs[5].append(ps["f_v"].reshape(db, dseq, fheads, hd))
        outs[6].append(ps["f_logf"].reshape(db, dseq, fheads))
        outs[7].append(s_s)

    stk = [jnp.stack(o) for o in outs]
    return (xp.reshape(batch, seq, d), xs.reshape(db, dseq, d), *stk)
```
